```python
import math
import jax, jax.numpy as jnp
from jax import lax
import numpy as np

D_MODEL = 2048
BATCH = 8
SEQ = 2048
DEPTH = 2
DEC_BATCH = 32
DEC_SEQ = 4
PAST_LEN = 8192
PAGE_SIZE = 128

N_MIXERS = 2
N_LAYERS_A = (DEPTH + 1) // 2
N_LAYERS_B = DEPTH // 2
BRANCH = D_MODEL
N_HEADS_A = 8
HEAD_DIM_A = BRANCH // (2 * N_HEADS_A)
VDIM_A = 2 * HEAD_DIM_A
N_HEADS_B = 16
HEAD_DIM_B = BRANCH // N_HEADS_B
N_BUCKETS = 32
MAX_DISTANCE = 128
Q_BLOCK = 128
EPS = 1e-6
NEG = -1e30
LAMBDA_STD = 0.1

kernel_name = "hybrid_diffattn_stickbreaking_decode_step"


def _rmsnorm(x, g):
    xf = x.astype(jnp.float32)
    y = xf * lax.rsqrt(jnp.mean(xf * xf, axis=-1, keepdims=True) + EPS)
    return (y * g.astype(jnp.float32)).astype(x.dtype)


def _rel_bucket(q_pos, k_pos):
    n = jnp.maximum(q_pos[:, None] - k_pos[None, :], 0)
    max_exact = N_BUCKETS // 2
    nf = jnp.maximum(n, 1).astype(jnp.float32)
    large = max_exact + (jnp.log(nf / max_exact) / math.log(MAX_DISTANCE / max_exact)
                         * (N_BUCKETS - max_exact)).astype(jnp.int32)
    large = jnp.minimum(large, N_BUCKETS - 1)
    return jnp.where(n < max_exact, n, large)


def _sweep(fn, q, q_pos):
    B, T = q.shape[0], q.shape[1]
    qb = Q_BLOCK if T % Q_BLOCK == 0 else T
    nb = T // qb
    qs = jnp.moveaxis(q.reshape((B, nb, qb) + q.shape[2:]), 1, 0)
    ps = q_pos.reshape(nb, qb)
    o = lax.map(lambda a: fn(a[0], a[1]), (qs, ps))
    o = jnp.moveaxis(o, 0, 1)
    return o.reshape((B, T) + o.shape[3:])


def _with_past(new, past, q_pos):
    if past is None:
        return new, q_pos
    p = past.shape[1]
    k_pos = jnp.concatenate([jnp.arange(p, dtype=jnp.int32), q_pos])
    return jnp.concatenate([past.astype(new.dtype), new], axis=1), k_pos


def _diff_mixer(h, q_pos, past_k, past_v, w_in, w_out, lq1, lk1, lq2, lk2, g_sub, rel_bias, lam_init):
    B, T, _ = h.shape
    q, k, v, gate = jnp.split(h @ w_in, 4, axis=-1)
    q = q.reshape(B, T, N_HEADS_A, 2, HEAD_DIM_A)
    k = k.reshape(B, T, N_HEADS_A, 2, HEAD_DIM_A)
    v = v.reshape(B, T, N_HEADS_A, VDIM_A)
    k_all, k_pos = _with_past(k, past_k, q_pos)
    v_all, _ = _with_past(v, past_v, q_pos)
    f32 = jnp.float32
    lam = (jnp.exp(jnp.sum(lq1.astype(f32) * lk1.astype(f32)))
           - jnp.exp(jnp.sum(lq2.astype(f32) * lk2.astype(f32))) + lam_init)
    scale = HEAD_DIM_A ** -0.5

    def block(qb_, pb_):
        s = jnp.einsum('bqhmd,bkhmd->bhmqk', qb_, k_all, preferred_element_type=f32) * scale
        bias = jnp.transpose(rel_bias.astype(f32)[_rel_bucket(pb_, k_pos)], (2, 0, 1))
        s = s + bias[None, :, None]
        mask = k_pos[None, :] <= pb_[:, None]
        p = jax.nn.softmax(jnp.where(mask, s, NEG), axis=-1)
        a = p[:, :, 0] - lam * p[:, :, 1]
        return jnp.einsum('bhqk,bkhe->bqhe', a.astype(v_all.dtype), v_all, preferred_element_type=f32)

    o = _sweep(block, q, q_pos)
    o = _rmsnorm(o, g_sub) * (1.0 - lam_init)
    o = o.reshape(B, T, BRANCH).astype(h.dtype) * jax.nn.silu(gate)
    return o @ w_out, k, v


def _sb_mixer(h, q_pos, past_k, past_v, w_in, w_out):
    B, T, _ = h.shape
    q, k, v, gate = jnp.split(h @ w_in, 4, axis=-1)
    q = q.reshape(B, T, N_HEADS_B, HEAD_DIM_B)
    k = k.reshape(B, T, N_HEADS_B, HEAD_DIM_B)
    v = v.reshape(B, T, N_HEADS_B, HEAD_DIM_B)
    k_all, k_pos = _with_past(k, past_k, q_pos)
    v_all, _ = _with_past(v, past_v, q_pos)
    f32 = jnp.float32
    scale = HEAD_DIM_B ** -0.5

    def block(qb_, pb_):
        z = jnp.einsum('bqhd,bkhd->bhqk', qb_, k_all, preferred_element_type=f32) * scale
        mask = k_pos[None, :] < pb_[:, None]
        sp = jnp.where(mask, jax.nn.softplus(z), 0.0)
        after = lax.cumsum(sp, axis=sp.ndim - 1, reverse=True) - sp
        log_a = jnp.where(mask, z - sp - after, NEG)
        a = jnp.exp(log_a)
        return jnp.einsum('bhqk,bkhd->bqhd', a.astype(v_all.dtype), v_all, preferred_element_type=f32)

    o = _sweep(block, q, q_pos)
    o = o.reshape(B, T, BRANCH).astype(h.dtype) * jax.nn.silu(gate)
    return o @ w_out, k, v


def _gather_pages(cache, page_table):
    g = cache[page_table]
    return g.reshape((page_table.shape[0], page_table.shape[1] * cache.shape[1]) + cache.shape[2:])


def _trunk(x, q_pos, caches, page_table, rel_bias, norm_pre, norm_post,
           w_in_diff, w_out_diff, lambda_q1, lambda_k1, lambda_q2, lambda_k2, subln_diff,
           w_in_sb, w_out_sb):
    cache_k_diff, cache_v_diff, cache_k_sb, cache_v_sb = caches
    kd, vd, ks, vs = [], [], [], []
    for i in range(DEPTH):
        j = i // N_MIXERS
        h = _rmsnorm(x, norm_pre[i])
        if i % N_MIXERS == 0:
            pk = None if page_table is None else _gather_pages(cache_k_diff[j], page_table)
            pv = None if page_table is None else _gather_pages(cache_v_diff[j], page_table)
            lam_init = 0.8 - 0.6 * math.exp(-0.3 * i)
            o, k, v = _diff_mixer(h, q_pos, pk, pv, w_in_diff[j], w_out_diff[j], lambda_q1[j], lambda_k1[j],
                                  lambda_q2[j], lambda_k2[j], subln_diff[j], rel_bias, lam_init)
            kd.append(k); vd.append(v)
        else:
            pk = None if page_table is None else _gather_pages(cache_k_sb[j], page_table)
            pv = None if page_table is None else _gather_pages(cache_v_sb[j], page_table)
            o, k, v = _sb_mixer(h, q_pos, pk, pv, w_in_sb[j], w_out_sb[j])
            ks.append(k); vs.append(v)
        x = x + _rmsnorm(o, norm_post[i])
    return x, jnp.stack(kd), jnp.stack(vd), jnp.stack(ks), jnp.stack(vs)


def setup_inputs(seed: int = 0) -> dict:
    key = jax.random.key(seed)
    ks = jax.random.split(key, 20)
    f32 = jnp.float32
    n_pages = PAST_LEN // PAGE_SIZE
    n_used = DEC_BATCH * n_pages
    n_pool = (n_used * 5) // 4
    nrm = lambda k, shape, s=1.0: (jax.random.normal(k, shape, f32) * s)
    page_table = jax.random.permutation(ks[6], n_pool)[:n_used].reshape(DEC_BATCH, n_pages).astype(jnp.int32)
    return {
        "x_prompt": nrm(ks[0], (BATCH, SEQ, D_MODEL)),
        "x_sample": nrm(ks[1], (DEC_BATCH, DEC_SEQ, D_MODEL)),
        "cache_k_diff": nrm(ks[2], (N_LAYERS_A, n_pool, PAGE_SIZE, N_HEADS_A, 2, HEAD_DIM_A)),
        "cache_v_diff": nrm(ks[3], (N_LAYERS_A, n_pool, PAGE_SIZE, N_HEADS_A, VDIM_A)),
        "cache_k_sb": nrm(ks[4], (N_LAYERS_B, n_pool, PAGE_SIZE, N_HEADS_B, HEAD_DIM_B)),
        "cache_v_sb": nrm(ks[5], (N_LAYERS_B, n_pool, PAGE_SIZE, N_HEADS_B, HEAD_DIM_B)),
        "page_table": page_table,
        "rel_bias": nrm(ks[7], (N_BUCKETS, N_HEADS_A), 0.5),
        "norm_pre": 1.0 + nrm(ks[8], (DEPTH, D_MODEL), 0.05),
        "norm_post": 1.0 + nrm(ks[9], (DEPTH, D_MODEL), 0.05),
        "w_in_diff": nrm(ks[10], (N_LAYERS_A, D_MODEL, 4 * BRANCH), D_MODEL ** -0.5),
        "w_out_diff": nrm(ks[11], (N_LAYERS_A, BRANCH, D_MODEL), BRANCH ** -0.5),
        "lambda_q1": nrm(ks[12], (N_LAYERS_A, HEAD_DIM_A), LAMBDA_STD),
        "lambda_k1": nrm(ks[13], (N_LAYERS_A, HEAD_DIM_A), LAMBDA_STD),
        "lambda_q2": nrm(ks[14], (N_LAYERS_A, HEAD_DIM_A), LAMBDA_STD),
        "lambda_k2": nrm(ks[15], (N_LAYERS_A, HEAD_DIM_A), LAMBDA_STD),
        "subln_diff": 1.0 + nrm(ks[16], (N_LAYERS_A, VDIM_A), 0.05),
        "w_in_sb": nrm(ks[17], (N_LAYERS_B, D_MODEL, 4 * BRANCH), D_MODEL ** -0.5),
        "w_out_sb": nrm(ks[18], (N_LAYERS_B, BRANCH, D_MODEL), BRANCH ** -0.5),
    }


def reference(x_prompt, x_sample, cache_k_diff, cache_v_diff, cache_k_sb, cache_v_sb, page_table,
              rel_bias, norm_pre, norm_post, w_in_diff, w_out_diff, lambda_q1, lambda_k1, lambda_q2,
              lambda_k2, subln_diff, w_in_sb, w_out_sb):
    caches = (cache_k_diff, cache_v_diff, cache_k_sb, cache_v_sb)
    weights = (rel_bias, norm_pre, norm_post, w_in_diff, w_out_diff, lambda_q1, lambda_k1,
               lambda_q2, lambda_k2, subln_diff, w_in_sb, w_out_sb)
    past_len = page_table.shape[1] * PAGE_SIZE
    pos_prompt = jnp.arange(x_prompt.shape[1], dtype=jnp.int32)
    pos_sample = past_len + jnp.arange(x_sample.shape[1], dtype=jnp.int32)
    y_prompt, k_diff_prompt, v_diff_prompt, k_sb_prompt, v_sb_prompt = _trunk(
        x_prompt, pos_prompt, caches, None, *weights)
    y_sample, k_diff_sample, v_diff_sample, k_sb_sample, v_sb_sample = _trunk(
        x_sample, pos_sample, caches, page_table, *weights)
    return (y_prompt, y_sample, k_diff_prompt, v_diff_prompt, k_sb_prompt, v_sb_prompt,
            k_diff_sample, v_diff_sample, k_sb_sample, v_sb_sample)
```

```python
import functools
import math

import jax
import jax.numpy as jnp
from jax import lax
from jax.experimental import pallas as pl
from jax.experimental.pallas import tpu as pltpu

F32 = jnp.float32
BF16 = jnp.bfloat16

N_HEADS_A = 8
HEAD_DIM_A = 128
VDIM_A = 2 * HEAD_DIM_A
N_HEADS_B = 16
HEAD_DIM_B = 128
N_GROUPS = 16
N_BUCKETS = 32
MAX_EXACT = N_BUCKETS // 2
MAX_DISTANCE = 128
EPS = 1e-6
NEG = -1e30

LANE = 128
VMEM_LIMIT = 56 * 1024 * 1024

Q_TILE = 256
PAGES_PER_STEP = 4
PROJ_ROWS = 1024
PROJ_COLS = 512
OUT_ROWS = 512


def _params(*sem):
    return pltpu.CompilerParams(dimension_semantics=sem, vmem_limit_bytes=VMEM_LIMIT)


def _dot_nt(a, b):
    return lax.dot_general(a, b, (((1,), (1,)), ((), ())), preferred_element_type=F32)


def _dot(a, b):
    return jnp.dot(a, b, preferred_element_type=F32)


def _silu(g):
    return g * (1.0 / (1.0 + jnp.exp(-g)))


def _softplus(z):
    return jnp.maximum(z, 0.0) + jnp.log(1.0 + jnp.exp(-jnp.abs(z)))


def _lam_init(layer):
    return 0.8 - 0.6 * math.exp(-0.3 * layer)


def _inproj_body(x_ref, g_ref, w_ref, q_ref, k_ref, v_ref, gate_ref, h_ref, *, n_sub, q_scale):
    j = pl.program_id(1)

    @pl.when(j == 0)
    def _():
        x = x_ref[...]
        ms = jnp.mean(x * x, axis=-1, keepdims=True)
        h_ref[...] = (x * lax.rsqrt(ms + EPS) * g_ref[...]).astype(BF16)

    acc = _dot(h_ref[...], w_ref[...])

    @pl.when(j < n_sub)
    def _():
        q_ref[...] = (acc * q_scale).astype(q_ref.dtype)

    @pl.when((j >= n_sub) & (j < 2 * n_sub))
    def _():
        k_ref[...] = acc

    @pl.when((j >= 2 * n_sub) & (j < 3 * n_sub))
    def _():
        v_ref[...] = acc

    @pl.when(j >= 3 * n_sub)
    def _():
        gate_ref[...] = acc.astype(gate_ref.dtype)


def _inproj(x, g, w, *, q_scale, gate_dtype):
    R, D = x.shape
    BR = w.shape[1] // 4
    tm = min(PROJ_ROWS, R)
    tn = PROJ_COLS
    assert R % tm == 0 and BR % tn == 0
    n_sub = BR // tn

    def out_map(group):
        return lambda i, j: (i, jnp.clip(j - group * n_sub, 0, n_sub - 1))

    return pl.pallas_call(
        functools.partial(_inproj_body, n_sub=n_sub, q_scale=q_scale),
        grid=(R // tm, 4 * n_sub),
        in_specs=[
            pl.BlockSpec((tm, D), lambda i, j: (i, 0)),
            pl.BlockSpec((1, D), lambda i, j: (0, 0)),
            pl.BlockSpec((D, tn), lambda i, j: (0, j)),
        ],
        out_specs=[
            pl.BlockSpec((tm, tn), out_map(0)),
            pl.BlockSpec((tm, tn), out_map(1)),
            pl.BlockSpec((tm, tn), out_map(2)),
            pl.BlockSpec((tm, tn), out_map(3)),
        ],
        out_shape=[
            jax.ShapeDtypeStruct((R, BR), BF16),
            jax.ShapeDtypeStruct((R, BR), F32),
            jax.ShapeDtypeStruct((R, BR), F32),
            jax.ShapeDtypeStruct((R, BR), gate_dtype),
        ],
        scratch_shapes=[pltpu.VMEM((tm, D), BF16)],
        compiler_params=_params("arbitrary", "arbitrary"),
        name="inproj",
    )(x, g.reshape(1, D), w)


def _outproj_body(o_ref, w_ref, g_ref, x_ref, y_ref):
    o = _dot(o_ref[...].astype(BF16), w_ref[...])
    ms = jnp.mean(o * o, axis=-1, keepdims=True)
    y_ref[...] = x_ref[...] + o * lax.rsqrt(ms + EPS) * g_ref[...]


def _outproj(o, w, g, x):
    R, BR = o.shape
    D = w.shape[1]
    tm = min(OUT_ROWS, R)
    assert R % tm == 0
    return pl.pallas_call(
        _outproj_body,
        grid=(R // tm,),
        in_specs=[
            pl.BlockSpec((tm, BR), lambda i: (i, 0)),
            pl.BlockSpec((BR, D), lambda i: (0, 0)),
            pl.BlockSpec((1, D), lambda i: (0, 0)),
            pl.BlockSpec((tm, D), lambda i: (i, 0)),
        ],
        out_specs=pl.BlockSpec((tm, D), lambda i: (i, 0)),
        out_shape=jax.ShapeDtypeStruct((R, D), F32),
        compiler_params=_params("arbitrary"),
        name="outproj",
    )(o, w, g.reshape(1, D), x)


def _bias_body(rb_ref, prev_ref, diag_ref, last_ref, new_ref, *, page, dec_seq):
    h = pl.program_id(0)

    def tile(shape, q_of_row, offset):
        row = lax.broadcasted_iota(jnp.int32, shape, 0)
        col = lax.broadcasted_iota(jnp.int32, shape, 1)
        d = q_of_row(row) + offset - col
        n = jnp.maximum(d, 0)
        nf = jnp.maximum(n, 1).astype(F32)
        large = MAX_EXACT + (jnp.log(nf / MAX_EXACT) / math.log(MAX_DISTANCE / MAX_EXACT)
                             * (N_BUCKETS - MAX_EXACT)).astype(jnp.int32)
        large = jnp.minimum(large, N_BUCKETS - 1)
        bucket = jnp.where(n < MAX_EXACT, n, large)
        out = jnp.zeros(shape, F32)
        for b in range(N_BUCKETS):
            out = jnp.where(bucket == b, rb_ref[h, b], out)
        out = out - rb_ref[h, N_BUCKETS - 1]
        return jnp.where(d >= 0, out, NEG)

    tq = prev_ref.shape[0]
    prev_ref[...] = tile((tq, tq), lambda r: r, tq)
    diag_ref[...] = tile((tq, tq), lambda r: r, 0)
    rows = last_ref.shape[0]
    last_ref[...] = tile((rows, page), lambda r: r % dec_seq, page)
    new_ref[...] = tile((rows, page), lambda r: r % dec_seq, 0)


def _bias_tiles(rel_bias, *, tq, page, dec_seq):
    H = rel_bias.shape[1]
    rows = 2 * dec_seq
    return pl.pallas_call(
        functools.partial(_bias_body, page=page, dec_seq=dec_seq),
        grid=(H,),
        in_specs=[pl.BlockSpec(memory_space=pltpu.SMEM)],
        out_specs=[
            pl.BlockSpec((None, tq, tq), lambda h: (h, 0, 0)),
            pl.BlockSpec((None, tq, tq), lambda h: (h, 0, 0)),
            pl.BlockSpec((None, rows, page), lambda h: (h, 0, 0)),
            pl.BlockSpec((None, rows, page), lambda h: (h, 0, 0)),
        ],
        out_shape=[
            jax.ShapeDtypeStruct((H, tq, tq), F32),
            jax.ShapeDtypeStruct((H, tq, tq), F32),
            jax.ShapeDtypeStruct((H, rows, page), F32),
            jax.ShapeDtypeStruct((H, rows, page), F32),
        ],
        compiler_params=_params("arbitrary"),
        name="bias_tiles",
    )(rel_bias.T)


def _lambda(lq1, lk1, lq2, lk2, lam_init):
    s1 = jnp.sum(lq1[...] * lk1[...], axis=-1, keepdims=True)
    s2 = jnp.sum(lq2[...] * lk2[...], axis=-1, keepdims=True)
    return jnp.exp(s1) - jnp.exp(s2) + lam_init


def _diff_epilogue(o, gate, gsub, lam_init):
    ms = jnp.mean(o * o, axis=-1, keepdims=True)
    on = (o * lax.rsqrt(ms + EPS) * gsub) * (1.0 - lam_init)
    return on * _silu(gate.astype(F32))


def _diff_prompt_body(q_ref, k_ref, v_ref, gate_ref, prev_ref, diag_ref,
                      lq1, lk1, lq2, lk2, gsub_ref, o_ref,
                      kb_ref, vb_ref, m_ref, l_ref, acc_ref, *, lam_init):
    qi = pl.program_id(2)
    tq = q_ref.shape[0]
    d = HEAD_DIM_A

    @pl.when(qi == 0)
    def _():
        kb_ref[...] = k_ref[...].astype(BF16)
        vb_ref[...] = v_ref[...].astype(BF16)

    m_ref[...] = jnp.full(m_ref.shape, NEG, F32)
    l_ref[...] = jnp.zeros(l_ref.shape, F32)
    acc_ref[...] = jnp.zeros(acc_ref.shape, F32)

    def update(start, bias_ref):
        vblk = vb_ref[pl.ds(start, tq), :]
        for mp in range(2):
            s = _dot_nt(q_ref[:, mp * d:(mp + 1) * d], kb_ref[pl.ds(start, tq), mp * d:(mp + 1) * d])
            if bias_ref is not None:
                s = s + bias_ref[...]
            m_old = m_ref[mp]
            m_new = jnp.maximum(m_old, jnp.max(s, axis=-1, keepdims=True))
            p = jnp.exp(s - m_new)
            alpha = jnp.exp(m_old - m_new)
            l_ref[mp] = alpha * l_ref[mp] + jnp.sum(p, axis=-1, keepdims=True)
            acc_ref[mp] = alpha * acc_ref[mp] + _dot(p.astype(BF16), vblk)
            m_ref[mp] = m_new

    def far(kc, carry):
        update(pl.multiple_of(kc * tq, tq), None)
        return carry

    lax.fori_loop(0, jnp.maximum(qi - 1, 0), far, 0)

    @pl.when(qi > 0)
    def _():
        update(pl.multiple_of((qi - 1) * tq, tq), prev_ref)

    update(pl.multiple_of(qi * tq, tq), diag_ref)

    lam = _lambda(lq1, lk1, lq2, lk2, lam_init)
    o = acc_ref[0] / l_ref[0] - lam * (acc_ref[1] / l_ref[1])
    o_ref[...] = _diff_epilogue(o, gate_ref[...], gsub_ref[...], lam_init).astype(o_ref.dtype)


def _diff_prompt(q, k, v, gate, prev, diag, lams, gsub, *, lam_init):
    B, T, BR = q.shape
    tq = Q_TILE
    assert T % tq == 0
    H, W = N_HEADS_A, VDIM_A
    vec = pl.BlockSpec((1, HEAD_DIM_A), lambda b, h, i: (0, 0))
    return pl.pallas_call(
        functools.partial(_diff_prompt_body, lam_init=lam_init),
        grid=(B, H, T // tq),
        in_specs=[
            pl.BlockSpec((None, tq, W), lambda b, h, i: (b, i, h)),
            pl.BlockSpec((None, T, W), lambda b, h, i: (b, 0, h)),
            pl.BlockSpec((None, T, W), lambda b, h, i: (b, 0, h)),
            pl.BlockSpec((None, tq, W), lambda b, h, i: (b, i, h)),
            pl.BlockSpec((None, tq, tq), lambda b, h, i: (h, 0, 0)),
            pl.BlockSpec((None, tq, tq), lambda b, h, i: (h, 0, 0)),
            vec, vec, vec, vec,
            pl.BlockSpec((1, W), lambda b, h, i: (0, 0)),
        ],
        out_specs=pl.BlockSpec((None, tq, W), lambda b, h, i: (b, i, h)),
        out_shape=jax.ShapeDtypeStruct((B, T, BR), BF16),
        scratch_shapes=[
            pltpu.VMEM((T, W), BF16),
            pltpu.VMEM((T, W), BF16),
            pltpu.VMEM((2, tq, 1), F32),
            pltpu.VMEM((2, tq, 1), F32),
            pltpu.VMEM((2, tq, W), F32),
        ],
        compiler_params=_params("arbitrary", "arbitrary", "arbitrary"),
        name="diff_prompt",
    )(q, k, v, gate, prev, diag, *lams, gsub)


def _split_bf16(x):
    hi = x.astype(BF16)
    lo = (x - hi.astype(F32)).astype(BF16)
    return hi, lo


def _suffix_sum(sp, tri):
    hi, lo = _split_bf16(sp)
    return _dot(hi, tri) + _dot(lo, tri)


def _sb_prompt_body(q_ref, k_ref, v_ref, gate_ref, tri_ref, o_ref,
                    kb_ref, vb_ref, carry_ref, acc_ref):
    qi = pl.program_id(2)
    tq = q_ref.shape[0]

    @pl.when(qi == 0)
    def _():
        kb_ref[...] = k_ref[...].astype(BF16)
        vb_ref[...] = v_ref[...].astype(BF16)

    carry_ref[...] = jnp.zeros(carry_ref.shape, F32)
    acc_ref[...] = jnp.zeros(acc_ref.shape, F32)

    def update(start, masked):
        z = _dot_nt(q_ref[...], kb_ref[pl.ds(start, tq), :])
        sp = _softplus(z)
        if masked:
            row = lax.broadcasted_iota(jnp.int32, z.shape, 0)
            col = lax.broadcasted_iota(jnp.int32, z.shape, 1)
            mask = col < row
            sp = jnp.where(mask, sp, 0.0)
        cum = _suffix_sum(sp, tri_ref[...])
        a = jnp.exp(z - cum - carry_ref[...])
        if masked:
            a = jnp.where(mask, a, 0.0)
        acc_ref[...] += _dot(a.astype(BF16), vb_ref[pl.ds(start, tq), :])
        carry_ref[...] += cum[:, 0:1]

    update(pl.multiple_of(qi * tq, tq), True)

    def back(i, c):
        update(pl.multiple_of((qi - 1 - i) * tq, tq), False)
        return c

    lax.fori_loop(0, qi, back, 0)

    o_ref[...] = (acc_ref[...] * _silu(gate_ref[...].astype(F32))).astype(o_ref.dtype)


def _tri(n):
    return jnp.tril(jnp.ones((n, n), BF16))


def _sb_prompt(q, k, v, gate):
    B, T, BR = q.shape
    tq = Q_TILE
    assert T % tq == 0
    H, W = N_HEADS_B, HEAD_DIM_B
    return pl.pallas_call(
        _sb_prompt_body,
        grid=(B, H, T // tq),
        in_specs=[
            pl.BlockSpec((None, tq, W), lambda b, h, i: (b, i, h)),
            pl.BlockSpec((None, T, W), lambda b, h, i: (b, 0, h)),
            pl.BlockSpec((None, T, W), lambda b, h, i: (b, 0, h)),
            pl.BlockSpec((None, tq, W), lambda b, h, i: (b, i, h)),
            pl.BlockSpec((tq, tq), lambda b, h, i: (0, 0)),
        ],
        out_specs=pl.BlockSpec((None, tq, W), lambda b, h, i: (b, i, h)),
        out_shape=jax.ShapeDtypeStruct((B, T, BR), BF16),
        scratch_shapes=[
            pltpu.VMEM((T, W), BF16),
            pltpu.VMEM((T, W), BF16),
            pltpu.VMEM((tq, 1), F32),
            pltpu.VMEM((tq, W), F32),
        ],
        compiler_params=_params("arbitrary", "arbitrary", "arbitrary"),
        name="sb_prompt",
    )(q, k, v, gate, _tri(tq))


def _block_diag_q(q, dec_seq):
    DB = q.shape[0]
    t = q.reshape(DB, dec_seq, N_GROUPS, LANE).transpose(0, 2, 1, 3)
    eye = jnp.eye(N_GROUPS, dtype=q.dtype)
    bd = t[:, :, :, None, :] * eye[None, :, None, :, None]
    return bd.reshape(DB, N_GROUPS * dec_seq, N_GROUPS * LANE)


def _page_specs(layer, pages_of_step, page, width):
    def spec(s):
        return pl.BlockSpec((None, None, page, width),
                            lambda b, g, pt: (layer, pt[b, pages_of_step(g, s)], 0, 0))
    return [spec(s) for s in range(PAGES_PER_STEP)]


def _new_page(new_ref, page):
    n = new_ref.shape[0]
    pad = jnp.zeros((page - n, new_ref.shape[1]), F32)
    return jnp.concatenate([new_ref[...], pad], axis=0)


def _diff_sample_body(pt_ref, q_ref, *refs, lam_init, n_steps):
    P = PAGES_PER_STEP
    k_refs, v_refs = refs[:P], refs[P:2 * P]
    (knew_ref, vnew_ref, last_ref, new_ref, gate_ref, lq1, lk1, lq2, lk2, gsub_ref,
     o_ref, m_ref, l_ref, acc_ref, r_ref) = refs[2 * P:]
    g = pl.program_id(1)
    W = VDIM_A
    rows_h = acc_ref.shape[0] // N_HEADS_A
    dec_seq = rows_h // 2

    @pl.when(g == 0)
    def _():
        m_ref[...] = jnp.full(m_ref.shape, NEG, F32)
        l_ref[...] = jnp.zeros(l_ref.shape, F32)
        acc_ref[...] = jnp.zeros(acc_ref.shape, F32)

    def update(kpage, vpage, bias):
        s = _dot_nt(q_ref[...], kpage.astype(BF16))
        if bias is not None:
            s = s + bias
        m_old = m_ref[...]
        m_new = jnp.maximum(m_old, jnp.max(s, axis=-1, keepdims=True))
        p = jnp.exp(s - m_new)
        alpha = jnp.exp(m_old - m_new)
        l_ref[...] = alpha * l_ref[...] + jnp.sum(p, axis=-1, keepdims=True)
        m_ref[...] = m_new
        pv = _dot(p.astype(BF16), vpage.astype(BF16))
        for h in range(N_HEADS_A):
            r0 = h * rows_h
            acc_ref[r0:r0 + rows_h, :] = (alpha[r0:r0 + rows_h] * acc_ref[r0:r0 + rows_h, :]
                                          + pv[r0:r0 + rows_h, h * W:(h + 1) * W])

    for s in range(P):
        bias = None
        if s == P - 1:
            bias = jnp.where(g == n_steps - 1, last_ref[...], 0.0)
        update(k_refs[s][...], v_refs[s][...], bias)

    @pl.when(g == n_steps - 1)
    def _():
        page = k_refs[0].shape[0]
        update(_new_page(knew_ref, page), _new_page(vnew_ref, page), new_ref[...])
        r_ref[...] = acc_ref[...] / l_ref[...]
        lam = _lambda(lq1, lk1, lq2, lk2, lam_init)
        for h in range(N_HEADS_A):
            r0 = h * rows_h
            o = r_ref[r0:r0 + dec_seq, :] - lam * r_ref[r0 + dec_seq:r0 + rows_h, :]
            o_ref[:, h * W:(h + 1) * W] = _diff_epilogue(
                o, gate_ref[:, h * W:(h + 1) * W], gsub_ref[...], lam_init)


def _diff_sample(qbd, cache_k, cache_v, layer, page_table, knew, vnew, last, new, gate, lams, gsub,
                 *, lam_init):
    DB, rows, BR = qbd.shape
    dec_seq = knew.shape[1]
    page = cache_k.shape[2]
    n_pages = page_table.shape[1]
    P = PAGES_PER_STEP
    assert n_pages % P == 0
    n_steps = n_pages // P
    full = lambda shape: pl.BlockSpec(shape, lambda b, g, pt: (0,) * len(shape))
    per_seq = lambda r: pl.BlockSpec((None, r, BR), lambda b, g, pt: (b, 0, 0))
    forward = lambda g, s: g * P + s
    grid_spec = pltpu.PrefetchScalarGridSpec(
        num_scalar_prefetch=1,
        grid=(DB, n_steps),
        in_specs=[per_seq(rows)]
        + _page_specs(layer, forward, page, BR) + _page_specs(layer, forward, page, BR)
        + [per_seq(dec_seq), per_seq(dec_seq), full((rows, page)), full((rows, page)), per_seq(dec_seq)]
        + [full((1, HEAD_DIM_A))] * 4 + [full((1, VDIM_A))],
        out_specs=per_seq(dec_seq),
        scratch_shapes=[
            pltpu.VMEM((rows, 1), F32),
            pltpu.VMEM((rows, 1), F32),
            pltpu.VMEM((rows, VDIM_A), F32),
            pltpu.VMEM((rows, VDIM_A), F32),
        ],
    )
    return pl.pallas_call(
        functools.partial(_diff_sample_body, lam_init=lam_init, n_steps=n_steps),
        grid_spec=grid_spec,
        out_shape=jax.ShapeDtypeStruct((DB, dec_seq, BR), F32),
        compiler_params=_params("arbitrary", "arbitrary"),
        name="diff_sample",
    )(page_table, qbd, *([cache_k] * P), *([cache_v] * P), knew, vnew, last, new, gate, *lams, gsub)


def _sb_sample_body(pt_ref, q_ref, *refs, n_steps):
    P = PAGES_PER_STEP
    k_refs, v_refs = refs[:P], refs[P:2 * P]
    knew_ref, vnew_ref, mask_ref, gate_ref, tri_ref, o_ref, carry_ref, acc_ref = refs[2 * P:]
    g = pl.program_id(1)
    W = HEAD_DIM_B
    dec_seq = o_ref.shape[0]
    n_pairs = N_HEADS_B // 2
    rows_p = 2 * dec_seq

    def update(kpage, vpage, mask):
        z = _dot_nt(q_ref[...], kpage.astype(BF16))
        sp = _softplus(z)
        if mask is not None:
            sp = sp * mask
        cum = _suffix_sum(sp, tri_ref[...])
        a = jnp.exp(z - cum - carry_ref[...])
        if mask is not None:
            a = a * mask
        pv = _dot(a.astype(BF16), vpage.astype(BF16))
        for hp in range(n_pairs):
            r0 = hp * rows_p
            acc_ref[r0:r0 + rows_p, :] += pv[r0:r0 + rows_p, hp * 2 * W:(hp + 1) * 2 * W]
        carry_ref[...] += cum[:, 0:1]

    @pl.when(g == 0)
    def _():
        carry_ref[...] = jnp.zeros(carry_ref.shape, F32)
        acc_ref[...] = jnp.zeros(acc_ref.shape, F32)
        page = k_refs[0].shape[0]
        update(_new_page(knew_ref, page), _new_page(vnew_ref, page), mask_ref[...])

    for s in range(P):
        update(k_refs[s][...], v_refs[s][...], None)

    @pl.when(g == n_steps - 1)
    def _():
        for hp in range(n_pairs):
            r0 = hp * rows_p
            for e in range(2):
                h = 2 * hp + e
                o = acc_ref[r0 + e * dec_seq:r0 + (e + 1) * dec_seq, e * W:(e + 1) * W]
                o_ref[:, h * W:(h + 1) * W] = o * _silu(gate_ref[:, h * W:(h + 1) * W].astype(F32))


def _sb_sample(qbd, cache_k, cache_v, layer, page_table, knew, vnew, gate):
    DB, rows, BR = qbd.shape
    dec_seq = knew.shape[1]
    page = cache_k.shape[2]
    n_pages = page_table.shape[1]
    P = PAGES_PER_STEP
    assert n_pages % P == 0
    n_steps = n_pages // P
    t_of_row = jnp.arange(rows, dtype=jnp.int32)[:, None] % dec_seq
    mask = (jnp.arange(page, dtype=jnp.int32)[None, :] < t_of_row).astype(F32)
    full = lambda shape: pl.BlockSpec(shape, lambda b, g, pt: (0,) * len(shape))
    per_seq = lambda r: pl.BlockSpec((None, r, BR), lambda b, g, pt: (b, 0, 0))
    backward = lambda g, s: n_pages - 1 - (g * P + s)
    grid_spec = pltpu.PrefetchScalarGridSpec(
        num_scalar_prefetch=1,
        grid=(DB, n_steps),
        in_specs=[per_seq(rows)]
        + _page_specs(layer, backward, page, BR) + _page_specs(layer, backward, page, BR)
        + [per_seq(dec_seq), per_seq(dec_seq), full((rows, page)), per_seq(dec_seq), full((page, page))],
        out_specs=per_seq(dec_seq),
        scratch_shapes=[
            pltpu.VMEM((rows, 1), F32),
            pltpu.VMEM((rows, 2 * HEAD_DIM_B), F32),
        ],
    )
    return pl.pallas_call(
        functools.partial(_sb_sample_body, n_steps=n_steps),
        grid_spec=grid_spec,
        out_shape=jax.ShapeDtypeStruct((DB, dec_seq, BR), F32),
        compiler_params=_params("arbitrary", "arbitrary"),
        name="sb_sample",
    )(page_table, qbd, *([cache_k] * P), *([cache_v] * P), knew, vnew, mask, gate, _tri(page))


def _trunk(x, caches, page_table, tiles, norm_pre, norm_post, w_in, w_out, lams, subln):
    B, T, D = x.shape
    depth = norm_pre.shape[0]
    prev, diag, last, new = tiles
    cache_k_diff, cache_v_diff, cache_k_sb, cache_v_sb = caches
    sample = page_table is not None
    xf = x.reshape(B * T, D)
    kd, vd, ks, vs = [], [], [], []
    for i in range(depth):
        j = i // 2
        diff = i % 2 == 0
        head_dim = HEAD_DIM_A if diff else HEAD_DIM_B
        q, k, v, gate = _inproj(xf, norm_pre[i], w_in[i], q_scale=head_dim ** -0.5,
                                gate_dtype=F32 if sample else BF16)
        BR = q.shape[1]
        shp = lambda a: a.reshape(B, T, BR)
        if diff:
            lam_init = _lam_init(i)
            lam_j = tuple(l[j].reshape(1, HEAD_DIM_A) for l in lams)
            gsub = subln[j].reshape(1, VDIM_A)
            if sample:
                o = _diff_sample(_block_diag_q(shp(q), T), cache_k_diff, cache_v_diff, j, page_table,
                                 shp(k), shp(v), last, new, shp(gate), lam_j, gsub, lam_init=lam_init)
            else:
                o = _diff_prompt(shp(q), shp(k), shp(v), shp(gate), prev, diag, lam_j, gsub,
                                 lam_init=lam_init)
            kd.append(k.reshape(B, T, N_HEADS_A, 2, HEAD_DIM_A))
            vd.append(v.reshape(B, T, N_HEADS_A, VDIM_A))
        else:
            if sample:
                o = _sb_sample(_block_diag_q(shp(q), T), cache_k_sb, cache_v_sb, j, page_table,
                               shp(k), shp(v), shp(gate))
            else:
                o = _sb_prompt(shp(q), shp(k), shp(v), shp(gate))
            ks.append(k.reshape(B, T, N_HEADS_B, HEAD_DIM_B))
            vs.append(v.reshape(B, T, N_HEADS_B, HEAD_DIM_B))
        xf = _outproj(o.reshape(B * T, BR), w_out[i], norm_post[i], xf)
    return xf.reshape(B, T, D), jnp.stack(kd), jnp.stack(vd), jnp.stack(ks), jnp.stack(vs)


def kernel(x_prompt, x_sample, cache_k_diff, cache_v_diff, cache_k_sb, cache_v_sb, page_table, rel_bias, norm_pre, norm_post, w_in_diff, w_out_diff, lambda_q1, lambda_k1, lambda_q2, lambda_k2, subln_diff, w_in_sb, w_out_sb):
    depth = norm_pre.shape[0]
    page = cache_k_diff.shape[2]
    dec_seq = x_sample.shape[1]
    w_in = [(w_in_diff if i % 2 == 0 else w_in_sb)[i // 2].astype(BF16) for i in range(depth)]
    w_out = [(w_out_diff if i % 2 == 0 else w_out_sb)[i // 2].astype(BF16) for i in range(depth)]
    flat = lambda c: c.reshape(c.shape[0], c.shape[1], c.shape[2], -1)
    caches = tuple(flat(c) for c in (cache_k_diff, cache_v_diff, cache_k_sb, cache_v_sb))
    prev, diag, last, new = _bias_tiles(rel_bias, tq=Q_TILE, page=page, dec_seq=dec_seq)
    rows = N_GROUPS * dec_seq
    tiles = (prev, diag, last.reshape(rows, page), new.reshape(rows, page))
    lams = (lambda_q1, lambda_k1, lambda_q2, lambda_k2)
    args = (tiles, norm_pre, norm_post, w_in, w_out, lams, subln_diff)
    y_p, kd_p, vd_p, ks_p, vs_p = _trunk(x_prompt, caches, None, *args)
    y_s, kd_s, vd_s, ks_s, vs_s = _trunk(x_sample, caches, page_table, *args)
    return (y_p, y_s, kd_p, vd_p, ks_p, vs_p, kd_s, vd_s, ks_s, vs_s)
```

```python
import functools
import math

import jax
import jax.numpy as jnp
from jax import lax
from jax.experimental import pallas as pl
from jax.experimental.pallas import tpu as pltpu

F32 = jnp.float32
BF16 = jnp.bfloat16

N_HEADS_A = 8
HEAD_DIM_A = 128
VDIM_A = 2 * HEAD_DIM_A
N_HEADS_B = 16
HEAD_DIM_B = 128
N_GROUPS = 16
N_BUCKETS = 32
MAX_EXACT = N_BUCKETS // 2
MAX_DISTANCE = 128
EPS = 1e-6
NEG = -1e30

LANE = 128
SUBLANE = 8
VMEM_LIMIT = 56 * 1024 * 1024

Q_TILE = 256
PAGES_PER_STEP = 4
PROJ_ROWS = 1024
PROJ_COLS = 512
OUT_ROWS = 512


def _params(*sem):
    return pltpu.CompilerParams(dimension_semantics=sem, vmem_limit_bytes=VMEM_LIMIT)


def _dot_nt(a, b):
    return lax.dot_general(a, b, (((1,), (1,)), ((), ())), preferred_element_type=F32)


def _dot(a, b):
    return jnp.dot(a, b, preferred_element_type=F32)


def _silu(g):
    return g * (1.0 / (1.0 + jnp.exp(-g)))


def _softplus(z):
    return jnp.maximum(z, 0.0) + jnp.log(1.0 + jnp.exp(-jnp.abs(z)))


def _lam_init(layer):
    return 0.8 - 0.6 * math.exp(-0.3 * layer)


def _inproj_body(x_ref, g_ref, w_ref, q_ref, k_ref, v_ref, gate_ref, h_ref, *, n_sub, q_scale):
    j = pl.program_id(1)

    @pl.when(j == 0)
    def _():
        x = x_ref[...]
        ms = jnp.mean(x * x, axis=-1, keepdims=True)
        h_ref[...] = (x * lax.rsqrt(ms + EPS) * g_ref[...]).astype(BF16)

    def project():
        return _dot(h_ref[...], w_ref[...])

    @pl.when(j < n_sub)
    def _():
        q_ref[...] = (project() * q_scale).astype(q_ref.dtype)

    @pl.when((j >= n_sub) & (j < 2 * n_sub))
    def _():
        k_ref[...] = project()

    @pl.when((j >= 2 * n_sub) & (j < 3 * n_sub))
    def _():
        v_ref[...] = project()

    @pl.when(j >= 3 * n_sub)
    def _():
        gate_ref[...] = project().astype(gate_ref.dtype)


def _inproj(x, g, w, *, q_scale, gate_dtype):
    R, D = x.shape
    BR = w.shape[1] // 4
    tm = min(PROJ_ROWS, R)
    tn = PROJ_COLS
    assert R % tm == 0 and BR % tn == 0
    n_sub = BR // tn

    def out_map(group):
        return lambda i, j: (i, jnp.clip(j - group * n_sub, 0, n_sub - 1))

    return pl.pallas_call(
        functools.partial(_inproj_body, n_sub=n_sub, q_scale=q_scale),
        grid=(R // tm, 4 * n_sub),
        in_specs=[
            pl.BlockSpec((tm, D), lambda i, j: (i, 0)),
            pl.BlockSpec((1, D), lambda i, j: (0, 0)),
            pl.BlockSpec((D, tn), lambda i, j: (0, j)),
        ],
        out_specs=[
            pl.BlockSpec((tm, tn), out_map(0)),
            pl.BlockSpec((tm, tn), out_map(1)),
            pl.BlockSpec((tm, tn), out_map(2)),
            pl.BlockSpec((tm, tn), out_map(3)),
        ],
        out_shape=[
            jax.ShapeDtypeStruct((R, BR), BF16),
            jax.ShapeDtypeStruct((R, BR), F32),
            jax.ShapeDtypeStruct((R, BR), F32),
            jax.ShapeDtypeStruct((R, BR), gate_dtype),
        ],
        scratch_shapes=[pltpu.VMEM((tm, D), BF16)],
        compiler_params=_params("arbitrary", "arbitrary"),
        name="inproj",
    )(x, g.reshape(1, D), w)


def _outproj_body(o_ref, w_ref, g_ref, x_ref, y_ref):
    o = _dot(o_ref[...].astype(BF16), w_ref[...])
    ms = jnp.mean(o * o, axis=-1, keepdims=True)
    y_ref[...] = x_ref[...] + o * lax.rsqrt(ms + EPS) * g_ref[...]


def _outproj(o, w, g, x):
    R, BR = o.shape
    D = w.shape[1]
    tm = min(OUT_ROWS, R)
    assert R % tm == 0
    return pl.pallas_call(
        _outproj_body,
        grid=(R // tm,),
        in_specs=[
            pl.BlockSpec((tm, BR), lambda i: (i, 0)),
            pl.BlockSpec((BR, D), lambda i: (0, 0)),
            pl.BlockSpec((1, D), lambda i: (0, 0)),
            pl.BlockSpec((tm, D), lambda i: (i, 0)),
        ],
        out_specs=pl.BlockSpec((tm, D), lambda i: (i, 0)),
        out_shape=jax.ShapeDtypeStruct((R, D), F32),
        compiler_params=_params("arbitrary"),
        name="outproj",
    )(o, w, g.reshape(1, D), x)


def _bias_body(rb_ref, near_ref, last_ref, new_ref, *, page, dec_seq):
    h = pl.program_id(0)

    def tile(shape, q_of_row, offset):
        row = lax.broadcasted_iota(jnp.int32, shape, 0)
        col = lax.broadcasted_iota(jnp.int32, shape, 1)
        d = q_of_row(row) + offset - col
        n = jnp.maximum(d, 0)
        nf = jnp.maximum(n, 1).astype(F32)
        large = MAX_EXACT + (jnp.log(nf / MAX_EXACT) / math.log(MAX_DISTANCE / MAX_EXACT)
                             * (N_BUCKETS - MAX_EXACT)).astype(jnp.int32)
        large = jnp.minimum(large, N_BUCKETS - 1)
        bucket = jnp.where(n < MAX_EXACT, n, large)
        out = jnp.zeros(shape, F32)
        for b in range(N_BUCKETS):
            out = jnp.where(bucket == b, rb_ref[h, b], out)
        out = out - rb_ref[h, N_BUCKETS - 1]
        return jnp.where(d >= 0, out, NEG)

    tq = near_ref.shape[0]
    near_ref[...] = tile((tq, 2 * tq), lambda r: r, tq)
    rows = last_ref.shape[0]
    last_ref[...] = tile((rows, page), lambda r: r % dec_seq, page)
    new_ref[...] = tile((rows, page), lambda r: r % dec_seq, 0)


def _bias_tiles(rel_bias, *, tq, page, dec_seq):
    H = rel_bias.shape[1]
    rows = 2 * dec_seq
    return pl.pallas_call(
        functools.partial(_bias_body, page=page, dec_seq=dec_seq),
        grid=(H,),
        in_specs=[pl.BlockSpec(memory_space=pltpu.SMEM)],
        out_specs=[
            pl.BlockSpec((None, tq, 2 * tq), lambda h: (h, 0, 0)),
            pl.BlockSpec((None, rows, page), lambda h: (h, 0, 0)),
            pl.BlockSpec((None, rows, page), lambda h: (h, 0, 0)),
        ],
        out_shape=[
            jax.ShapeDtypeStruct((H, tq, 2 * tq), F32),
            jax.ShapeDtypeStruct((H, rows, page), F32),
            jax.ShapeDtypeStruct((H, rows, page), F32),
        ],
        compiler_params=_params("arbitrary"),
        name="bias_tiles",
    )(rel_bias.T)


def _lambda(lq1, lk1, lq2, lk2, lam_init):
    s1 = jnp.sum(lq1[...] * lk1[...], axis=-1, keepdims=True)
    s2 = jnp.sum(lq2[...] * lk2[...], axis=-1, keepdims=True)
    return jnp.exp(s1) - jnp.exp(s2) + lam_init


def _diff_epilogue(o, gate, gsub, lam_init):
    ms = jnp.mean(o * o, axis=-1, keepdims=True)
    on = (o * lax.rsqrt(ms + EPS) * gsub) * (1.0 - lam_init)
    return on * _silu(gate.astype(F32))


def _per_query_tile(qi, n_tiles, fn):
    for j in range(n_tiles):
        pl.when(qi == j)(functools.partial(fn, j))


def _diff_prompt_body(q_ref, k_ref, v_ref, gate_ref, near_ref,
                      lq1, lk1, lq2, lk2, gsub_ref, o_ref, kb_ref, vb_ref, *, lam_init):
    qi = pl.program_id(2)
    tq = q_ref.shape[0]
    d = HEAD_DIM_A

    @pl.when(qi == 0)
    def _():
        kb_ref[...] = k_ref[...].astype(BF16)
        vb_ref[...] = v_ref[...].astype(BF16)

    def tile(j):
        nk = (j + 1) * tq
        nn = min(nk, 2 * tq)
        nf = nk - nn
        probs = []
        for mp in range(2):
            cols = slice(mp * d, (mp + 1) * d)
            qm = q_ref[:, cols]
            sn = _dot_nt(qm, kb_ref[nf:nk, cols]) + near_ref[:, 2 * tq - nn:]
            m = jnp.max(sn, axis=-1, keepdims=True)
            if nf:
                sf = _dot_nt(qm, kb_ref[0:nf, cols])
                m = jnp.maximum(m, jnp.max(sf, axis=-1, keepdims=True))
            pn = jnp.exp(sn - m)
            l = jnp.sum(pn, axis=-1, keepdims=True)
            pf = None
            if nf:
                pf = jnp.exp(sf - m)
                l = l + jnp.sum(pf, axis=-1, keepdims=True)
            probs.append((pn, pf, l))
        lam = _lambda(lq1, lk1, lq2, lk2, lam_init)
        (pn0, pf0, l0), (pn1, pf1, l1) = probs
        c0 = 1.0 / l0
        c1 = lam / l1
        o = _dot((pn0 * c0 - pn1 * c1).astype(BF16), vb_ref[nf:nk, :])
        if nf:
            o = o + _dot((pf0 * c0 - pf1 * c1).astype(BF16), vb_ref[0:nf, :])
        o_ref[...] = _diff_epilogue(o, gate_ref[...], gsub_ref[...], lam_init).astype(o_ref.dtype)

    _per_query_tile(qi, pl.num_programs(2), tile)


def _diff_prompt(q, k, v, gate, near, lams, gsub, *, lam_init):
    B, T, BR = q.shape
    tq = Q_TILE
    assert T % tq == 0
    H, W = N_HEADS_A, VDIM_A
    vec = pl.BlockSpec((1, HEAD_DIM_A), lambda b, h, i: (0, 0))
    return pl.pallas_call(
        functools.partial(_diff_prompt_body, lam_init=lam_init),
        grid=(B, H, T // tq),
        in_specs=[
            pl.BlockSpec((None, tq, W), lambda b, h, i: (b, i, h)),
            pl.BlockSpec((None, T, W), lambda b, h, i: (b, 0, h)),
            pl.BlockSpec((None, T, W), lambda b, h, i: (b, 0, h)),
            pl.BlockSpec((None, tq, W), lambda b, h, i: (b, i, h)),
            pl.BlockSpec((None, tq, 2 * tq), lambda b, h, i: (h, 0, 0)),
            vec, vec, vec, vec,
            pl.BlockSpec((1, W), lambda b, h, i: (0, 0)),
        ],
        out_specs=pl.BlockSpec((None, tq, W), lambda b, h, i: (b, i, h)),
        out_shape=jax.ShapeDtypeStruct((B, T, BR), BF16),
        scratch_shapes=[
            pltpu.VMEM((T, W), BF16),
            pltpu.VMEM((T, W), BF16),
        ],
        compiler_params=_params("arbitrary", "arbitrary", "arbitrary"),
        name="diff_prompt",
    )(q, k, v, gate, near, *lams, gsub)


def _split_bf16(x):
    hi = x.astype(BF16)
    lo = (x - hi.astype(F32)).astype(BF16)
    return hi, lo


def _suffix_sum(sp, tri2):
    return _dot(jnp.concatenate(_split_bf16(sp), axis=1), tri2)


def _sb_prompt_body(q_ref, k_ref, v_ref, gate_ref, tri_ref, o_ref, kb_ref, vb_ref):
    qi = pl.program_id(2)
    tq = q_ref.shape[0]

    @pl.when(qi == 0)
    def _():
        kb_ref[...] = k_ref[...].astype(BF16)
        vb_ref[...] = v_ref[...].astype(BF16)

    def tile(j):
        nk = (j + 1) * tq
        tri2 = tri_ref[...]
        row = lax.broadcasted_iota(jnp.int32, (tq, tq), 0)
        col = lax.broadcasted_iota(jnp.int32, (tq, tq), 1)
        mask = col < row
        z_all = _dot_nt(q_ref[...], kb_ref[0:nk, :])
        carry = None
        a_tiles = []
        for c in range(j, -1, -1):
            z = z_all[:, c * tq:(c + 1) * tq]
            sp = _softplus(z)
            if c == j:
                sp = jnp.where(mask, sp, 0.0)
            log_a = z - _suffix_sum(sp, tri2)
            if carry is not None:
                log_a = log_a - carry
            a = jnp.exp(log_a)
            if c == j:
                a = jnp.where(mask, a, 0.0)
            a_tiles.append(a.astype(BF16))
            if c:
                tot = jnp.sum(sp, axis=-1, keepdims=True)
                carry = tot if carry is None else carry + tot
        acc = _dot(jnp.concatenate(a_tiles[::-1], axis=1), vb_ref[0:nk, :])
        o_ref[...] = (acc * _silu(gate_ref[...].astype(F32))).astype(o_ref.dtype)

    _per_query_tile(qi, pl.num_programs(2), tile)


def _tri2(n):
    tri = jnp.tril(jnp.ones((n, n), BF16))
    return jnp.concatenate([tri, tri], axis=0)


def _sb_prompt(q, k, v, gate):
    B, T, BR = q.shape
    tq = Q_TILE
    assert T % tq == 0
    H, W = N_HEADS_B, HEAD_DIM_B
    return pl.pallas_call(
        _sb_prompt_body,
        grid=(B, H, T // tq),
        in_specs=[
            pl.BlockSpec((None, tq, W), lambda b, h, i: (b, i, h)),
            pl.BlockSpec((None, T, W), lambda b, h, i: (b, 0, h)),
            pl.BlockSpec((None, T, W), lambda b, h, i: (b, 0, h)),
            pl.BlockSpec((None, tq, W), lambda b, h, i: (b, i, h)),
            pl.BlockSpec((2 * tq, tq), lambda b, h, i: (0, 0)),
        ],
        out_specs=pl.BlockSpec((None, tq, W), lambda b, h, i: (b, i, h)),
        out_shape=jax.ShapeDtypeStruct((B, T, BR), BF16),
        scratch_shapes=[
            pltpu.VMEM((T, W), BF16),
            pltpu.VMEM((T, W), BF16),
        ],
        compiler_params=_params("arbitrary", "arbitrary", "arbitrary"),
        name="sb_prompt",
    )(q, k, v, gate, _tri2(tq))


def _block_diag_q(q, dec_seq):
    DB = q.shape[0]
    t = q.reshape(DB, dec_seq, N_GROUPS, LANE).transpose(0, 2, 1, 3)
    eye = jnp.eye(N_GROUPS, dtype=q.dtype)
    bd = t[:, :, :, None, :] * eye[None, :, None, :, None]
    return bd.reshape(DB, N_GROUPS * dec_seq, N_GROUPS * LANE)


def _half_page_specs(layer, pages_of_step, page, lane_halves):
    def spec(s, half):
        def index(b, g, pt):
            p = pt[b, pages_of_step(g, s)]
            return (layer, p, 0, 0, half) if lane_halves else (layer, p, 0, half, 0)
        return pl.BlockSpec((None, None, page, SUBLANE, LANE), index)
    return [spec(s, half) for s in range(PAGES_PER_STEP) for half in range(2)]


def _regroup(dst_ref, slot, halves, column_of):
    page = halves[0].shape[0]
    for half, ref in enumerate(halves):
        rows = ref.reshape(page * SUBLANE, LANE)
        for r in range(SUBLANE):
            c = column_of(half, r)
            dst_ref[slot * page:(slot + 1) * page, c * LANE:(c + 1) * LANE] = (
                rows[pl.ds(r, page, stride=SUBLANE), :].astype(BF16))


def _new_page(new_ref, page):
    n = new_ref.shape[0]
    pad = jnp.zeros((page - n, new_ref.shape[1]), F32)
    return jnp.concatenate([new_ref[...], pad], axis=0).astype(BF16)


def _diff_sample_body(pt_ref, q_ref, *refs, lam_init, n_steps):
    P = PAGES_PER_STEP
    k_refs, v_refs = refs[:2 * P], refs[2 * P:4 * P]
    (knew_ref, vnew_ref, last_ref, new_ref, gate_ref, lq1, lk1, lq2, lk2, gsub_ref,
     o_ref, kp_ref, vp_ref, m_ref, l_ref, acc_ref, r_ref) = refs[4 * P:]
    g = pl.program_id(1)
    W = VDIM_A
    rows_h = acc_ref.shape[0] // N_HEADS_A
    dec_seq = rows_h // 2

    @pl.when(g == 0)
    def _():
        m_ref[...] = jnp.full(m_ref.shape, NEG, F32)
        l_ref[...] = jnp.zeros(l_ref.shape, F32)
        acc_ref[...] = jnp.zeros(acc_ref.shape, F32)

    def update(kpage, vpage, bias):
        s = _dot_nt(q_ref[...], kpage)
        if bias is not None:
            s = s + bias
        m_old = m_ref[...]
        m_new = jnp.maximum(m_old, jnp.max(s, axis=-1, keepdims=True))
        p = jnp.exp(s - m_new)
        alpha = jnp.exp(m_old - m_new)
        l_ref[...] = alpha * l_ref[...] + jnp.sum(p, axis=-1, keepdims=True)
        m_ref[...] = m_new
        pv = _dot(p.astype(BF16), vpage)
        for h in range(N_HEADS_A):
            r0 = h * rows_h
            acc_ref[r0:r0 + rows_h, :] = (alpha[r0:r0 + rows_h] * acc_ref[r0:r0 + rows_h, :]
                                          + pv[r0:r0 + rows_h, h * W:(h + 1) * W])

    for s in range(P):
        _regroup(kp_ref, s, k_refs[2 * s:2 * s + 2], lambda half, r: half * SUBLANE + r)
        _regroup(vp_ref, s, v_refs[2 * s:2 * s + 2], lambda half, r: 2 * r + half)
    update(kp_ref[...], vp_ref[...], jnp.where(g == n_steps - 1, last_ref[...], 0.0))

    @pl.when(g == n_steps - 1)
    def _():
        page = k_refs[0].shape[0]
        update(_new_page(knew_ref, page), _new_page(vnew_ref, page), new_ref[...])
        r_ref[...] = acc_ref[...] / l_ref[...]
        lam = _lambda(lq1, lk1, lq2, lk2, lam_init)
        for h in range(N_HEADS_A):
            r0 = h * rows_h
            o = r_ref[r0:r0 + dec_seq, :] - lam * r_ref[r0 + dec_seq:r0 + rows_h, :]
            o_ref[:, h * W:(h + 1) * W] = _diff_epilogue(
                o, gate_ref[:, h * W:(h + 1) * W], gsub_ref[...], lam_init)


def _diff_sample(qbd, cache_k, cache_v, layer, page_table, knew, vnew, last, new, gate, lams, gsub,
                 *, lam_init):
    DB, rows, BR = qbd.shape
    dec_seq = knew.shape[1]
    page = cache_k.shape[2]
    n_pages = page_table.shape[1]
    P = PAGES_PER_STEP
    assert n_pages % P == 0
    n_steps = n_pages // P
    full = lambda shape: pl.BlockSpec(shape, lambda b, g, pt: (0,) * len(shape))
    per_seq = lambda r: pl.BlockSpec((None, r, BR), lambda b, g, pt: (b, 0, 0))
    forward = lambda g, s: g * P + s
    last = jnp.pad(last, ((0, 0), ((P - 1) * page, 0)))
    grid_spec = pltpu.PrefetchScalarGridSpec(
        num_scalar_prefetch=1,
        grid=(DB, n_steps),
        in_specs=[per_seq(rows)]
        + _half_page_specs(layer, forward, page, False) + _half_page_specs(layer, forward, page, True)
        + [per_seq(dec_seq), per_seq(dec_seq), full((rows, P * page)), full((rows, page)), per_seq(dec_seq)]
        + [full((1, HEAD_DIM_A))] * 4 + [full((1, VDIM_A))],
        out_specs=per_seq(dec_seq),
        scratch_shapes=[
            pltpu.VMEM((P * page, BR), BF16),
            pltpu.VMEM((P * page, BR), BF16),
            pltpu.VMEM((rows, 1), F32),
            pltpu.VMEM((rows, 1), F32),
            pltpu.VMEM((rows, VDIM_A), F32),
            pltpu.VMEM((rows, VDIM_A), F32),
        ],
    )
    return pl.pallas_call(
        functools.partial(_diff_sample_body, lam_init=lam_init, n_steps=n_steps),
        grid_spec=grid_spec,
        out_shape=jax.ShapeDtypeStruct((DB, dec_seq, BR), F32),
        compiler_params=_params("arbitrary", "arbitrary"),
        name="diff_sample",
    )(page_table, qbd, *([cache_k] * (2 * P)), *([cache_v] * (2 * P)),
      knew, vnew, last, new, gate, *lams, gsub)


def _sb_sample_body(pt_ref, q_ref, *refs, n_steps):
    P = PAGES_PER_STEP
    k_refs, v_refs = refs[:2 * P], refs[2 * P:4 * P]
    (knew_ref, vnew_ref, mask_ref, gate_ref, tri_step_ref, tri_page_ref, o_ref,
     kp_ref, vp_ref, carry_ref, acc_ref) = refs[4 * P:]
    g = pl.program_id(1)
    W = HEAD_DIM_B
    dec_seq = o_ref.shape[0]
    n_pairs = N_HEADS_B // 2
    rows_p = 2 * dec_seq

    def update(kpage, vpage, mask, tri2_ref):
        z = _dot_nt(q_ref[...], kpage)
        sp = _softplus(z)
        if mask is not None:
            sp = sp * mask
        cum = _suffix_sum(sp, tri2_ref[...])
        a = jnp.exp(z - cum - carry_ref[...])
        if mask is not None:
            a = a * mask
        pv = _dot(a.astype(BF16), vpage)
        for hp in range(n_pairs):
            r0 = hp * rows_p
            acc_ref[r0:r0 + rows_p, :] += pv[r0:r0 + rows_p, hp * 2 * W:(hp + 1) * 2 * W]
        carry_ref[...] += cum[:, 0:1]

    @pl.when(g == 0)
    def _():
        carry_ref[...] = jnp.zeros(carry_ref.shape, F32)
        acc_ref[...] = jnp.zeros(acc_ref.shape, F32)
        page = k_refs[0].shape[0]
        update(_new_page(knew_ref, page), _new_page(vnew_ref, page), mask_ref[...], tri_page_ref)

    by_head = lambda half, r: half * SUBLANE + r
    for s in range(P):
        _regroup(kp_ref, P - 1 - s, k_refs[2 * s:2 * s + 2], by_head)
        _regroup(vp_ref, P - 1 - s, v_refs[2 * s:2 * s + 2], by_head)
    update(kp_ref[...], vp_ref[...], None, tri_step_ref)

    @pl.when(g == n_steps - 1)
    def _():
        for hp in range(n_pairs):
            r0 = hp * rows_p
            for e in range(2):
                h = 2 * hp + e
                o = acc_ref[r0 + e * dec_seq:r0 + (e + 1) * dec_seq, e * W:(e + 1) * W]
                o_ref[:, h * W:(h + 1) * W] = o * _silu(gate_ref[:, h * W:(h + 1) * W].astype(F32))


def _sb_sample(qbd, cache_k, cache_v, layer, page_table, knew, vnew, gate):
    DB, rows, BR = qbd.shape
    dec_seq = knew.shape[1]
    page = cache_k.shape[2]
    n_pages = page_table.shape[1]
    P = PAGES_PER_STEP
    assert n_pages % P == 0
    n_steps = n_pages // P
    t_of_row = jnp.arange(rows, dtype=jnp.int32)[:, None] % dec_seq
    mask = (jnp.arange(page, dtype=jnp.int32)[None, :] < t_of_row).astype(F32)
    full = lambda shape: pl.BlockSpec(shape, lambda b, g, pt: (0,) * len(shape))
    per_seq = lambda r: pl.BlockSpec((None, r, BR), lambda b, g, pt: (b, 0, 0))
    backward = lambda g, s: n_pages - 1 - (g * P + s)
    grid_spec = pltpu.PrefetchScalarGridSpec(
        num_scalar_prefetch=1,
        grid=(DB, n_steps),
        in_specs=[per_seq(rows)]
        + _half_page_specs(layer, backward, page, False) + _half_page_specs(layer, backward, page, False)
        + [per_seq(dec_seq), per_seq(dec_seq), full((rows, page)), per_seq(dec_seq),
           full((2 * P * page, P * page)), full((2 * page, page))],
        out_specs=per_seq(dec_seq),
        scratch_shapes=[
            pltpu.VMEM((P * page, BR), BF16),
            pltpu.VMEM((P * page, BR), BF16),
            pltpu.VMEM((rows, 1), F32),
            pltpu.VMEM((rows, 2 * HEAD_DIM_B), F32),
        ],
    )
    return pl.pallas_call(
        functools.partial(_sb_sample_body, n_steps=n_steps),
        grid_spec=grid_spec,
        out_shape=jax.ShapeDtypeStruct((DB, dec_seq, BR), F32),
        compiler_params=_params("arbitrary", "arbitrary"),
        name="sb_sample",
    )(page_table, qbd, *([cache_k] * (2 * P)), *([cache_v] * (2 * P)), knew, vnew, mask, gate,
      _tri2(P * page), _tri2(page))


def _trunk(x, caches, page_table, tiles, norm_pre, norm_post, w_in, w_out, lams, subln):
    B, T, D = x.shape
    depth = norm_pre.shape[0]
    near, last, new = tiles
    cache_k_diff, cache_v_diff, cache_k_sb, cache_v_sb = caches
    sample = page_table is not None
    xf = x.reshape(B * T, D)
    kd, vd, ks, vs = [], [], [], []
    for i in range(depth):
        j = i // 2
        diff = i % 2 == 0
        head_dim = HEAD_DIM_A if diff else HEAD_DIM_B
        q, k, v, gate = _inproj(xf, norm_pre[i], w_in[i], q_scale=head_dim ** -0.5,
                                gate_dtype=F32 if sample else BF16)
        BR = q.shape[1]
        shp = lambda a: a.reshape(B, T, BR)
        if diff:
            lam_init = _lam_init(i)
            lam_j = tuple(l[j].reshape(1, HEAD_DIM_A) for l in lams)
            gsub = subln[j].reshape(1, VDIM_A)
            if sample:
                o = _diff_sample(_block_diag_q(shp(q), T), cache_k_diff, cache_v_diff, j, page_table,
                                 shp(k), shp(v), last, new, shp(gate), lam_j, gsub, lam_init=lam_init)
            else:
                o = _diff_prompt(shp(q), shp(k), shp(v), shp(gate), near, lam_j, gsub,
                                 lam_init=lam_init)
            kd.append(k.reshape(B, T, N_HEADS_A, 2, HEAD_DIM_A))
            vd.append(v.reshape(B, T, N_HEADS_A, VDIM_A))
        else:
            if sample:
                o = _sb_sample(_block_diag_q(shp(q), T), cache_k_sb, cache_v_sb, j, page_table,
                               shp(k), shp(v), shp(gate))
            else:
                o = _sb_prompt(shp(q), shp(k), shp(v), shp(gate))
            ks.append(k.reshape(B, T, N_HEADS_B, HEAD_DIM_B))
            vs.append(v.reshape(B, T, N_HEADS_B, HEAD_DIM_B))
        xf = _outproj(o.reshape(B * T, BR), w_out[i], norm_post[i], xf)
    return xf.reshape(B, T, D), jnp.stack(kd), jnp.stack(vd), jnp.stack(ks), jnp.stack(vs)


def kernel(x_prompt, x_sample, cache_k_diff, cache_v_diff, cache_k_sb, cache_v_sb, page_table, rel_bias, norm_pre, norm_post, w_in_diff, w_out_diff, lambda_q1, lambda_k1, lambda_q2, lambda_k2, subln_diff, w_in_sb, w_out_sb):
    depth = norm_pre.shape[0]
    page = cache_k_diff.shape[2]
    dec_seq = x_sample.shape[1]
    w_in = [(w_in_diff if i % 2 == 0 else w_in_sb)[i // 2].astype(BF16) for i in range(depth)]
    w_out = [(w_out_diff if i % 2 == 0 else w_out_sb)[i // 2].astype(BF16) for i in range(depth)]
    kd_shape = cache_k_diff.shape
    caches = (cache_k_diff.reshape(kd_shape[:3] + (N_GROUPS, LANE)), cache_v_diff, cache_k_sb, cache_v_sb)
    near, last, new = _bias_tiles(rel_bias, tq=Q_TILE, page=page, dec_seq=dec_seq)
    rows = N_GROUPS * dec_seq
    tiles = (near, last.reshape(rows, page), new.reshape(rows, page))
    lams = (lambda_q1, lambda_k1, lambda_q2, lambda_k2)
    args = (tiles, norm_pre, norm_post, w_in, w_out, lams, subln_diff)
    y_p, kd_p, vd_p, ks_p, vs_p = _trunk(x_prompt, caches, None, *args)
    y_s, kd_s, vd_s, ks_s, vs_s = _trunk(x_sample, caches, page_table, *args)
    return (y_p, y_s, kd_p, vd_p, ks_p, vs_p, kd_s, vd_s, ks_s, vs_s)
```

```python
import functools
import math

import jax
import jax.numpy as jnp
from jax import lax
from jax.experimental import pallas as pl
from jax.experimental.pallas import tpu as pltpu

F32 = jnp.float32
BF16 = jnp.bfloat16

N_HEADS_A = 8
HEAD_DIM_A = 128
VDIM_A = 2 * HEAD_DIM_A
N_HEADS_B = 16
HEAD_DIM_B = 128
N_GROUPS = 16
N_BUCKETS = 32
MAX_EXACT = N_BUCKETS // 2
MAX_DISTANCE = 128
EPS = 1e-6
NEG = -1e30

LANE = 128
SUBLANE = 8
VMEM_LIMIT = 56 * 1024 * 1024

Q_TILE = 256
PAGES_PER_STEP = 8
PAGES_PER_HALF = PAGES_PER_STEP // 2
PROJ_ROWS = 1024
PROJ_COLS = 512
OUT_ROWS = 512


def _params(*sem):
    return pltpu.CompilerParams(dimension_semantics=sem, vmem_limit_bytes=VMEM_LIMIT)


def _dot_nt(a, b):
    return lax.dot_general(a, b, (((1,), (1,)), ((), ())), preferred_element_type=F32)


def _dot(a, b):
    return jnp.dot(a, b, preferred_element_type=F32)


def _silu(g):
    return g * (1.0 / (1.0 + jnp.exp(-g)))


def _softplus(z):
    return jnp.maximum(z, 0.0) + jnp.log(1.0 + jnp.exp(-jnp.abs(z)))


def _lam_init(layer):
    return 0.8 - 0.6 * math.exp(-0.3 * layer)


def _inproj_body(x_ref, g_ref, w_ref, q_ref, k_ref, v_ref, gate_ref, h_ref, *, n_sub, q_scale):
    j = pl.program_id(1)

    @pl.when(j == 0)
    def _():
        x = x_ref[...]
        ms = jnp.mean(x * x, axis=-1, keepdims=True)
        h_ref[...] = (x * lax.rsqrt(ms + EPS) * g_ref[...]).astype(BF16)

    def project():
        return _dot(h_ref[...], w_ref[...])

    @pl.when(j < n_sub)
    def _():
        q_ref[...] = (project() * q_scale).astype(q_ref.dtype)

    @pl.when((j >= n_sub) & (j < 2 * n_sub))
    def _():
        k_ref[...] = project()

    @pl.when((j >= 2 * n_sub) & (j < 3 * n_sub))
    def _():
        v_ref[...] = project()

    @pl.when(j >= 3 * n_sub)
    def _():
        gate_ref[...] = project().astype(gate_ref.dtype)


def _inproj(x, g, w, *, q_scale, gate_dtype):
    R, D = x.shape
    BR = w.shape[1] // 4
    tm = min(PROJ_ROWS, R)
    tn = PROJ_COLS
    assert R % tm == 0 and BR % tn == 0
    n_sub = BR // tn

    def out_map(group):
        return lambda i, j: (i, jnp.clip(j - group * n_sub, 0, n_sub - 1))

    return pl.pallas_call(
        functools.partial(_inproj_body, n_sub=n_sub, q_scale=q_scale),
        grid=(R // tm, 4 * n_sub),
        in_specs=[
            pl.BlockSpec((tm, D), lambda i, j: (i, 0)),
            pl.BlockSpec((1, D), lambda i, j: (0, 0)),
            pl.BlockSpec((D, tn), lambda i, j: (0, j)),
        ],
        out_specs=[
            pl.BlockSpec((tm, tn), out_map(0)),
            pl.BlockSpec((tm, tn), out_map(1)),
            pl.BlockSpec((tm, tn), out_map(2)),
            pl.BlockSpec((tm, tn), out_map(3)),
        ],
        out_shape=[
            jax.ShapeDtypeStruct((R, BR), BF16),
            jax.ShapeDtypeStruct((R, BR), F32),
            jax.ShapeDtypeStruct((R, BR), F32),
            jax.ShapeDtypeStruct((R, BR), gate_dtype),
        ],
        scratch_shapes=[pltpu.VMEM((tm, D), BF16)],
        compiler_params=_params("arbitrary", "arbitrary"),
        name="inproj",
    )(x, g.reshape(1, D), w)


def _outproj_body(o_ref, w_ref, g_ref, x_ref, y_ref):
    o = _dot(o_ref[...].astype(BF16), w_ref[...])
    ms = jnp.mean(o * o, axis=-1, keepdims=True)
    y_ref[...] = x_ref[...] + o * lax.rsqrt(ms + EPS) * g_ref[...]


def _outproj(o, w, g, x):
    R, BR = o.shape
    D = w.shape[1]
    tm = min(OUT_ROWS, R)
    assert R % tm == 0
    return pl.pallas_call(
        _outproj_body,
        grid=(R // tm,),
        in_specs=[
            pl.BlockSpec((tm, BR), lambda i: (i, 0)),
            pl.BlockSpec((BR, D), lambda i: (0, 0)),
            pl.BlockSpec((1, D), lambda i: (0, 0)),
            pl.BlockSpec((tm, D), lambda i: (i, 0)),
        ],
        out_specs=pl.BlockSpec((tm, D), lambda i: (i, 0)),
        out_shape=jax.ShapeDtypeStruct((R, D), F32),
        compiler_params=_params("arbitrary"),
        name="outproj",
    )(o, w, g.reshape(1, D), x)


def _bias_body(rb_ref, near_ref, last_ref, new_ref, *, page, dec_seq):
    h = pl.program_id(0)

    def tile(shape, q_of_row, offset):
        row = lax.broadcasted_iota(jnp.int32, shape, 0)
        col = lax.broadcasted_iota(jnp.int32, shape, 1)
        d = q_of_row(row) + offset - col
        n = jnp.maximum(d, 0)
        nf = jnp.maximum(n, 1).astype(F32)
        large = MAX_EXACT + (jnp.log(nf / MAX_EXACT) / math.log(MAX_DISTANCE / MAX_EXACT)
                             * (N_BUCKETS - MAX_EXACT)).astype(jnp.int32)
        large = jnp.minimum(large, N_BUCKETS - 1)
        bucket = jnp.where(n < MAX_EXACT, n, large)
        out = jnp.zeros(shape, F32)
        for b in range(N_BUCKETS):
            out = jnp.where(bucket == b, rb_ref[h, b], out)
        out = out - rb_ref[h, N_BUCKETS - 1]
        return jnp.where(d >= 0, out, NEG)

    tq = near_ref.shape[0]
    near_ref[...] = tile((tq, 2 * tq), lambda r: r, tq)
    rows = last_ref.shape[0]
    last_ref[...] = tile((rows, page), lambda r: r % dec_seq, page)
    new_ref[...] = tile((rows, page), lambda r: r % dec_seq, 0)


def _bias_tiles(rel_bias, *, tq, page, dec_seq):
    H = rel_bias.shape[1]
    rows = 2 * dec_seq
    return pl.pallas_call(
        functools.partial(_bias_body, page=page, dec_seq=dec_seq),
        grid=(H,),
        in_specs=[pl.BlockSpec(memory_space=pltpu.SMEM)],
        out_specs=[
            pl.BlockSpec((None, tq, 2 * tq), lambda h: (h, 0, 0)),
            pl.BlockSpec((None, rows, page), lambda h: (h, 0, 0)),
            pl.BlockSpec((None, rows, page), lambda h: (h, 0, 0)),
        ],
        out_shape=[
            jax.ShapeDtypeStruct((H, tq, 2 * tq), F32),
            jax.ShapeDtypeStruct((H, rows, page), F32),
            jax.ShapeDtypeStruct((H, rows, page), F32),
        ],
        compiler_params=_params("arbitrary"),
        name="bias_tiles",
    )(rel_bias.T)


def _lambda(lq1, lk1, lq2, lk2, lam_init):
    s1 = jnp.sum(lq1[...] * lk1[...], axis=-1, keepdims=True)
    s2 = jnp.sum(lq2[...] * lk2[...], axis=-1, keepdims=True)
    return jnp.exp(s1) - jnp.exp(s2) + lam_init


def _diff_epilogue(o, gate, gsub, lam_init):
    ms = jnp.mean(o * o, axis=-1, keepdims=True)
    on = (o * lax.rsqrt(ms + EPS) * gsub) * (1.0 - lam_init)
    return on * _silu(gate.astype(F32))


def _per_query_tile(qi, n_tiles, fn):
    for j in range(n_tiles):
        pl.when(qi == j)(functools.partial(fn, j))


def _diff_prompt_body(q_ref, k_ref, v_ref, gate_ref, near_ref,
                      lq1, lk1, lq2, lk2, gsub_ref, o_ref, kb_ref, vb_ref, *, lam_init):
    qi = pl.program_id(2)
    tq = q_ref.shape[0]
    d = HEAD_DIM_A

    @pl.when(qi == 0)
    def _():
        kb_ref[...] = k_ref[...].astype(BF16)
        vb_ref[...] = v_ref[...].astype(BF16)

    def tile(j):
        nk = (j + 1) * tq
        nn = min(nk, 2 * tq)
        nf = nk - nn
        probs = []
        for mp in range(2):
            cols = slice(mp * d, (mp + 1) * d)
            qm = q_ref[:, cols]
            sn = _dot_nt(qm, kb_ref[nf:nk, cols]) + near_ref[:, 2 * tq - nn:]
            m = jnp.max(sn, axis=-1, keepdims=True)
            if nf:
                sf = _dot_nt(qm, kb_ref[0:nf, cols])
                m = jnp.maximum(m, jnp.max(sf, axis=-1, keepdims=True))
            pn = jnp.exp(sn - m)
            l = jnp.sum(pn, axis=-1, keepdims=True)
            pf = None
            if nf:
                pf = jnp.exp(sf - m)
                l = l + jnp.sum(pf, axis=-1, keepdims=True)
            probs.append((pn, pf, l))
        lam = _lambda(lq1, lk1, lq2, lk2, lam_init)
        (pn0, pf0, l0), (pn1, pf1, l1) = probs
        c0 = 1.0 / l0
        c1 = lam / l1
        o = _dot((pn0 * c0 - pn1 * c1).astype(BF16), vb_ref[nf:nk, :])
        if nf:
            o = o + _dot((pf0 * c0 - pf1 * c1).astype(BF16), vb_ref[0:nf, :])
        o_ref[...] = _diff_epilogue(o, gate_ref[...], gsub_ref[...], lam_init).astype(o_ref.dtype)

    _per_query_tile(qi, pl.num_programs(2), tile)


def _diff_prompt(q, k, v, gate, near, lams, gsub, *, lam_init):
    B, T, BR = q.shape
    tq = Q_TILE
    assert T % tq == 0
    H, W = N_HEADS_A, VDIM_A
    vec = pl.BlockSpec((1, HEAD_DIM_A), lambda b, h, i: (0, 0))
    return pl.pallas_call(
        functools.partial(_diff_prompt_body, lam_init=lam_init),
        grid=(B, H, T // tq),
        in_specs=[
            pl.BlockSpec((None, tq, W), lambda b, h, i: (b, i, h)),
            pl.BlockSpec((None, T, W), lambda b, h, i: (b, 0, h)),
            pl.BlockSpec((None, T, W), lambda b, h, i: (b, 0, h)),
            pl.BlockSpec((None, tq, W), lambda b, h, i: (b, i, h)),
            pl.BlockSpec((None, tq, 2 * tq), lambda b, h, i: (h, 0, 0)),
            vec, vec, vec, vec,
            pl.BlockSpec((1, W), lambda b, h, i: (0, 0)),
        ],
        out_specs=pl.BlockSpec((None, tq, W), lambda b, h, i: (b, i, h)),
        out_shape=jax.ShapeDtypeStruct((B, T, BR), BF16),
        scratch_shapes=[
            pltpu.VMEM((T, W), BF16),
            pltpu.VMEM((T, W), BF16),
        ],
        compiler_params=_params("arbitrary", "arbitrary", "arbitrary"),
        name="diff_prompt",
    )(q, k, v, gate, near, *lams, gsub)


def _suffix_sum(sp, tri):
    return _dot(sp.astype(BF16), tri)


def _sb_prompt_body(q_ref, k_ref, v_ref, gate_ref, tri_ref, o_ref, kb_ref, vb_ref):
    qi = pl.program_id(2)
    tq = q_ref.shape[0]

    @pl.when(qi == 0)
    def _():
        kb_ref[...] = k_ref[...].astype(BF16)
        vb_ref[...] = v_ref[...].astype(BF16)

    def head(j, cols):
        nk = (j + 1) * tq
        tri = tri_ref[...]
        row = lax.broadcasted_iota(jnp.int32, (tq, tq), 0)
        col = lax.broadcasted_iota(jnp.int32, (tq, tq), 1)
        mask = col < row
        z_all = _dot_nt(q_ref[:, cols], kb_ref[0:nk, cols])
        carry = None
        a_tiles = []
        for c in range(j, -1, -1):
            z = z_all[:, c * tq:(c + 1) * tq]
            sp = _softplus(z)
            if c == j:
                sp = jnp.where(mask, sp, 0.0)
            log_a = z - _suffix_sum(sp, tri)
            if carry is not None:
                log_a = log_a - carry
            a = jnp.exp(log_a)
            if c == j:
                a = jnp.where(mask, a, 0.0)
            a_tiles.append(a.astype(BF16))
            if c:
                tot = jnp.sum(sp, axis=-1, keepdims=True)
                carry = tot if carry is None else carry + tot
        acc = _dot(jnp.concatenate(a_tiles[::-1], axis=1), vb_ref[0:nk, cols])
        o_ref[:, cols] = (acc * _silu(gate_ref[:, cols].astype(F32))).astype(o_ref.dtype)

    def tile(j):
        for hh in range(q_ref.shape[1] // HEAD_DIM_B):
            head(j, slice(hh * HEAD_DIM_B, (hh + 1) * HEAD_DIM_B))

    _per_query_tile(qi, pl.num_programs(2), tile)


def _tri(n):
    return jnp.tril(jnp.ones((n, n), BF16))


def _sb_prompt(q, k, v, gate):
    B, T, BR = q.shape
    tq = Q_TILE
    assert T % tq == 0
    H, W = N_HEADS_B // 2, 2 * HEAD_DIM_B
    return pl.pallas_call(
        _sb_prompt_body,
        grid=(B, H, T // tq),
        in_specs=[
            pl.BlockSpec((None, tq, W), lambda b, h, i: (b, i, h)),
            pl.BlockSpec((None, T, W), lambda b, h, i: (b, 0, h)),
            pl.BlockSpec((None, T, W), lambda b, h, i: (b, 0, h)),
            pl.BlockSpec((None, tq, W), lambda b, h, i: (b, i, h)),
            pl.BlockSpec((tq, tq), lambda b, h, i: (0, 0)),
        ],
        out_specs=pl.BlockSpec((None, tq, W), lambda b, h, i: (b, i, h)),
        out_shape=jax.ShapeDtypeStruct((B, T, BR), BF16),
        scratch_shapes=[
            pltpu.VMEM((T, W), BF16),
            pltpu.VMEM((T, W), BF16),
        ],
        compiler_params=_params("arbitrary", "arbitrary", "arbitrary"),
        name="sb_prompt",
    )(q, k, v, gate, _tri(tq))


def _block_diag_q(q, dec_seq):
    DB = q.shape[0]
    t = q.reshape(DB, dec_seq, N_GROUPS, LANE).transpose(0, 2, 1, 3)
    eye = jnp.eye(N_GROUPS, dtype=q.dtype)
    bd = t[:, :, :, None, :] * eye[None, :, None, :, None]
    return bd.reshape(DB, N_GROUPS * dec_seq, N_GROUPS * LANE)


def _half_page_specs(layer, pages_of_step, page, lane_halves):
    def spec(s, half):
        def index(b, g, pt):
            p = pt[b, pages_of_step(g, s)]
            return (layer, p, 0, 0, half) if lane_halves else (layer, p, 0, half, 0)
        return pl.BlockSpec((None, None, page, SUBLANE, LANE), index)
    return [spec(s, half) for s in range(PAGES_PER_STEP) for half in range(2)]


def _regroup(dst_ref, slot, halves, column_of):
    page = halves[0].shape[0]
    for half, ref in enumerate(halves):
        rows = ref.reshape(page * SUBLANE, LANE)
        for r in range(SUBLANE):
            c = column_of(half, r)
            dst_ref[slot * page:(slot + 1) * page, c * LANE:(c + 1) * LANE] = (
                rows[pl.ds(r, page, stride=SUBLANE), :].astype(BF16))


def _new_page(new_ref, page):
    n = new_ref.shape[0]
    pad = jnp.zeros((page - n, new_ref.shape[1]), F32)
    return jnp.concatenate([new_ref[...], pad], axis=0).astype(BF16)


def _diff_sample_body(pt_ref, q_ref, *refs, lam_init, n_steps):
    P = PAGES_PER_STEP
    k_refs, v_refs = refs[:2 * P], refs[2 * P:4 * P]
    (knew_ref, vnew_ref, last_ref, new_ref, gate_ref, lq1, lk1, lq2, lk2, gsub_ref,
     o_ref, kp_ref, vp_ref, m_ref, l_ref, acc_ref, r_ref) = refs[4 * P:]
    g = pl.program_id(1)
    W = VDIM_A
    rows_h = acc_ref.shape[0] // N_HEADS_A
    dec_seq = rows_h // 2

    @pl.when((pl.program_id(0) == 0) & (g == 0))
    def _():
        kp_ref[1] = jnp.zeros(kp_ref.shape[1:], BF16)
        vp_ref[1] = jnp.zeros(vp_ref.shape[1:], BF16)

    @pl.when(g == 0)
    def _():
        m_ref[...] = jnp.full(m_ref.shape, NEG, F32)
        l_ref[...] = jnp.zeros(l_ref.shape, F32)
        acc_ref[...] = jnp.zeros(acc_ref.shape, F32)

    def regroup(half):
        for s in range(PAGES_PER_HALF):
            slot = half * PAGES_PER_HALF + s
            _regroup(kp_ref.at[half], s, k_refs[2 * slot:2 * slot + 2], lambda hf, r: hf * SUBLANE + r)
            _regroup(vp_ref.at[half], s, v_refs[2 * slot:2 * slot + 2], lambda hf, r: 2 * r + hf)

    def update(kpage, vpage, bias):
        s = _dot_nt(q_ref[...], kpage)
        if bias is not None:
            s = s + bias
        m_old = m_ref[...]
        m_new = jnp.maximum(m_old, jnp.max(s, axis=-1, keepdims=True))
        p = jnp.exp(s - m_new)
        alpha = jnp.exp(m_old - m_new)
        l_ref[...] = alpha * l_ref[...] + jnp.sum(p, axis=-1, keepdims=True)
        m_ref[...] = m_new
        pv = _dot(p.astype(BF16), vpage)
        for h in range(N_HEADS_A):
            r0 = h * rows_h
            acc_ref[r0:r0 + rows_h, :] = (alpha[r0:r0 + rows_h] * acc_ref[r0:r0 + rows_h, :]
                                          + pv[r0:r0 + rows_h, h * W:(h + 1) * W])

    update(kp_ref[1], vp_ref[1], jnp.where(g == 0, NEG, 0.0))
    regroup(0)
    regroup(1)
    update(kp_ref[0], vp_ref[0], None)

    @pl.when(g == n_steps - 1)
    def _():
        page = k_refs[0].shape[0]
        update(kp_ref[1], vp_ref[1], last_ref[...])
        update(_new_page(knew_ref, page), _new_page(vnew_ref, page), new_ref[...])
        r_ref[...] = acc_ref[...] / l_ref[...]
        lam = _lambda(lq1, lk1, lq2, lk2, lam_init)
        for h in range(N_HEADS_A):
            r0 = h * rows_h
            o = r_ref[r0:r0 + dec_seq, :] - lam * r_ref[r0 + dec_seq:r0 + rows_h, :]
            o_ref[:, h * W:(h + 1) * W] = _diff_epilogue(
                o, gate_ref[:, h * W:(h + 1) * W], gsub_ref[...], lam_init)


def _diff_sample(qbd, cache_k, cache_v, layer, page_table, knew, vnew, last, new, gate, lams, gsub,
                 *, lam_init):
    DB, rows, BR = qbd.shape
    dec_seq = knew.shape[1]
    page = cache_k.shape[2]
    n_pages = page_table.shape[1]
    P = PAGES_PER_STEP
    assert n_pages % P == 0
    n_steps = n_pages // P
    full = lambda shape: pl.BlockSpec(shape, lambda b, g, pt: (0,) * len(shape))
    per_seq = lambda r: pl.BlockSpec((None, r, BR), lambda b, g, pt: (b, 0, 0))
    forward = lambda g, s: g * P + s
    half_keys = PAGES_PER_HALF * page
    last = jnp.pad(last, ((0, 0), (half_keys - page, 0)))
    grid_spec = pltpu.PrefetchScalarGridSpec(
        num_scalar_prefetch=1,
        grid=(DB, n_steps),
        in_specs=[per_seq(rows)]
        + _half_page_specs(layer, forward, page, False) + _half_page_specs(layer, forward, page, True)
        + [per_seq(dec_seq), per_seq(dec_seq), full((rows, half_keys)), full((rows, page)), per_seq(dec_seq)]
        + [full((1, HEAD_DIM_A))] * 4 + [full((1, VDIM_A))],
        out_specs=per_seq(dec_seq),
        scratch_shapes=[
            pltpu.VMEM((2, half_keys, BR), BF16),
            pltpu.VMEM((2, half_keys, BR), BF16),
            pltpu.VMEM((rows, 1), F32),
            pltpu.VMEM((rows, 1), F32),
            pltpu.VMEM((rows, VDIM_A), F32),
            pltpu.VMEM((rows, VDIM_A), F32),
        ],
    )
    return pl.pallas_call(
        functools.partial(_diff_sample_body, lam_init=lam_init, n_steps=n_steps),
        grid_spec=grid_spec,
        out_shape=jax.ShapeDtypeStruct((DB, dec_seq, BR), F32),
        compiler_params=_params("arbitrary", "arbitrary"),
        name="diff_sample",
    )(page_table, qbd, *([cache_k] * (2 * P)), *([cache_v] * (2 * P)),
      knew, vnew, last, new, gate, *lams, gsub)


def _sb_sample_body(pt_ref, q_ref, *refs, n_steps):
    P = PAGES_PER_STEP
    k_refs, v_refs = refs[:2 * P], refs[2 * P:4 * P]
    (knew_ref, vnew_ref, mask_ref, gate_ref, tri_step_ref, tri_page_ref, o_ref,
     kp_ref, vp_ref, carry_ref, acc_ref) = refs[4 * P:]
    g = pl.program_id(1)
    W = HEAD_DIM_B
    dec_seq = o_ref.shape[0]
    n_pairs = N_HEADS_B // 2
    rows_p = 2 * dec_seq

    def update(kpage, vpage, mask, tri_ref, stale=None):
        z = _dot_nt(q_ref[...], kpage)
        sp = _softplus(z)
        if mask is not None:
            sp = sp * mask
        cum = _suffix_sum(sp, tri_ref[...])
        log_a = z - cum - carry_ref[...]
        if stale is not None:
            log_a = log_a + jnp.where(stale, NEG, 0.0)
        a = jnp.exp(log_a)
        if mask is not None:
            a = a * mask
        pv = _dot(a.astype(BF16), vpage)
        for hp in range(n_pairs):
            r0 = hp * rows_p
            acc_ref[r0:r0 + rows_p, :] += pv[r0:r0 + rows_p, hp * 2 * W:(hp + 1) * 2 * W]
        tot = cum[:, 0:1]
        if stale is not None:
            tot = tot * jnp.where(stale, 0.0, 1.0)
        carry_ref[...] += tot

    @pl.when((pl.program_id(0) == 0) & (g == 0))
    def _():
        kp_ref[1] = jnp.zeros(kp_ref.shape[1:], BF16)
        vp_ref[1] = jnp.zeros(vp_ref.shape[1:], BF16)

    @pl.when(g == 0)
    def _():
        carry_ref[...] = jnp.zeros(carry_ref.shape, F32)
        acc_ref[...] = jnp.zeros(acc_ref.shape, F32)
        page = k_refs[0].shape[0]
        update(_new_page(knew_ref, page), _new_page(vnew_ref, page), mask_ref[...], tri_page_ref)

    def regroup(half):
        for s in range(PAGES_PER_HALF):
            slot = half * PAGES_PER_HALF + s
            by_head = lambda hf, r: hf * SUBLANE + r
            _regroup(kp_ref.at[half], PAGES_PER_HALF - 1 - s, k_refs[2 * slot:2 * slot + 2], by_head)
            _regroup(vp_ref.at[half], PAGES_PER_HALF - 1 - s, v_refs[2 * slot:2 * slot + 2], by_head)

    update(kp_ref[1], vp_ref[1], None, tri_step_ref, stale=g == 0)
    regroup(0)
    regroup(1)
    update(kp_ref[0], vp_ref[0], None, tri_step_ref)

    @pl.when(g == n_steps - 1)
    def _():
        update(kp_ref[1], vp_ref[1], None, tri_step_ref)
        for hp in range(n_pairs):
            r0 = hp * rows_p
            for e in range(2):
                h = 2 * hp + e
                o = acc_ref[r0 + e * dec_seq:r0 + (e + 1) * dec_seq, e * W:(e + 1) * W]
                o_ref[:, h * W:(h + 1) * W] = o * _silu(gate_ref[:, h * W:(h + 1) * W].astype(F32))


def _sb_sample(qbd, cache_k, cache_v, layer, page_table, knew, vnew, gate):
    DB, rows, BR = qbd.shape
    dec_seq = knew.shape[1]
    page = cache_k.shape[2]
    n_pages = page_table.shape[1]
    P = PAGES_PER_STEP
    assert n_pages % P == 0
    n_steps = n_pages // P
    half_keys = PAGES_PER_HALF * page
    t_of_row = jnp.arange(rows, dtype=jnp.int32)[:, None] % dec_seq
    mask = (jnp.arange(page, dtype=jnp.int32)[None, :] < t_of_row).astype(F32)
    full = lambda shape: pl.BlockSpec(shape, lambda b, g, pt: (0,) * len(shape))
    per_seq = lambda r: pl.BlockSpec((None, r, BR), lambda b, g, pt: (b, 0, 0))
    backward = lambda g, s: n_pages - 1 - (g * P + s)
    grid_spec = pltpu.PrefetchScalarGridSpec(
        num_scalar_prefetch=1,
        grid=(DB, n_steps),
        in_specs=[per_seq(rows)]
        + _half_page_specs(layer, backward, page, False) + _half_page_specs(layer, backward, page, False)
        + [per_seq(dec_seq), per_seq(dec_seq), full((rows, page)), per_seq(dec_seq),
           full((half_keys, half_keys)), full((page, page))],
        out_specs=per_seq(dec_seq),
        scratch_shapes=[
            pltpu.VMEM((2, half_keys, BR), BF16),
            pltpu.VMEM((2, half_keys, BR), BF16),
            pltpu.VMEM((rows, 1), F32),
            pltpu.VMEM((rows, 2 * HEAD_DIM_B), F32),
        ],
    )
    return pl.pallas_call(
        functools.partial(_sb_sample_body, n_steps=n_steps),
        grid_spec=grid_spec,
        out_shape=jax.ShapeDtypeStruct((DB, dec_seq, BR), F32),
        compiler_params=_params("arbitrary", "arbitrary"),
        name="sb_sample",
    )(page_table, qbd, *([cache_k] * (2 * P)), *([cache_v] * (2 * P)), knew, vnew, mask, gate,
      _tri(half_keys), _tri(page))


def _trunk(x, caches, page_table, tiles, norm_pre, norm_post, w_in, w_out, lams, subln):
    B, T, D = x.shape
    depth = norm_pre.shape[0]
    near, last, new = tiles
    cache_k_diff, cache_v_diff, cache_k_sb, cache_v_sb = caches
    sample = page_table is not None
    xf = x.reshape(B * T, D)
    kd, vd, ks, vs = [], [], [], []
    for i in range(depth):
        j = i // 2
        diff = i % 2 == 0
        head_dim = HEAD_DIM_A if diff else HEAD_DIM_B
        q, k, v, gate = _inproj(xf, norm_pre[i], w_in[i], q_scale=head_dim ** -0.5,
                                gate_dtype=F32 if sample else BF16)
        BR = q.shape[1]
        shp = lambda a: a.reshape(B, T, BR)
        if diff:
            lam_init = _lam_init(i)
            lam_j = tuple(l[j].reshape(1, HEAD_DIM_A) for l in lams)
            gsub = subln[j].reshape(1, VDIM_A)
            if sample:
                o = _diff_sample(_block_diag_q(shp(q), T), cache_k_diff, cache_v_diff, j, page_table,
                                 shp(k), shp(v), last, new, shp(gate), lam_j, gsub, lam_init=lam_init)
            else:
                o = _diff_prompt(shp(q), shp(k), shp(v), shp(gate), near, lam_j, gsub,
                                 lam_init=lam_init)
            kd.append(k.reshape(B, T, N_HEADS_A, 2, HEAD_DIM_A))
            vd.append(v.reshape(B, T, N_HEADS_A, VDIM_A))
        else:
            if sample:
                o = _sb_sample(_block_diag_q(shp(q), T), cache_k_sb, cache_v_sb, j, page_table,
                               shp(k), shp(v), shp(gate))
            else:
                o = _sb_prompt(shp(q), shp(k), shp(v), shp(gate))
            ks.append(k.reshape(B, T, N_HEADS_B, HEAD_DIM_B))
            vs.append(v.reshape(B, T, N_HEADS_B, HEAD_DIM_B))
        xf = _outproj(o.reshape(B * T, BR), w_out[i], norm_post[i], xf)
    return xf.reshape(B, T, D), jnp.stack(kd), jnp.stack(vd), jnp.stack(ks), jnp.stack(vs)


def kernel(x_prompt, x_sample, cache_k_diff, cache_v_diff, cache_k_sb, cache_v_sb, page_table, rel_bias, norm_pre, norm_post, w_in_diff, w_out_diff, lambda_q1, lambda_k1, lambda_q2, lambda_k2, subln_diff, w_in_sb, w_out_sb):
    depth = norm_pre.shape[0]
    page = cache_k_diff.shape[2]
    dec_seq = x_sample.shape[1]
    w_in = [(w_in_diff if i % 2 == 0 else w_in_sb)[i // 2].astype(BF16) for i in range(depth)]
    w_out = [(w_out_diff if i % 2 == 0 else w_out_sb)[i // 2].astype(BF16) for i in range(depth)]
    kd_shape = cache_k_diff.shape
    caches = (cache_k_diff.reshape(kd_shape[:3] + (N_GROUPS, LANE)), cache_v_diff, cache_k_sb, cache_v_sb)
    near, last, new = _bias_tiles(rel_bias, tq=Q_TILE, page=page, dec_seq=dec_seq)
    rows = N_GROUPS * dec_seq
    tiles = (near, last.reshape(rows, page), new.reshape(rows, page))
    lams = (lambda_q1, lambda_k1, lambda_q2, lambda_k2)
    args = (tiles, norm_pre, norm_post, w_in, w_out, lams, subln_diff)
    y_p, kd_p, vd_p, ks_p, vs_p = _trunk(x_prompt, caches, None, *args)
    y_s, kd_s, vd_s, ks_s, vs_s = _trunk(x_sample, caches, page_table, *args)
    return (y_p, y_s, kd_p, vd_p, ks_p, vs_p, kd_s, vd_s, ks_s, vs_s)
```

```python
import functools
import math

import jax
import jax.numpy as jnp
from jax import lax
from jax.experimental import pallas as pl
from jax.experimental.pallas import tpu as pltpu

F32 = jnp.float32
BF16 = jnp.bfloat16

N_HEADS_A = 8
HEAD_DIM_A = 128
VDIM_A = 2 * HEAD_DIM_A
N_HEADS_B = 16
HEAD_DIM_B = 128
N_GROUPS = 16
N_BUCKETS = 32
MAX_EXACT = N_BUCKETS // 2
MAX_DISTANCE = 128
EPS = 1e-6
NEG = -1e30
LOG2E = math.log2(math.e)

LANE = 128
SUBLANE = 8
VMEM_LIMIT = 56 * 1024 * 1024

Q_TILE = 256
PAGES_PER_STEP = 8
PAGES_PER_HALF = PAGES_PER_STEP // 2
PROJ_ROWS = 1024
PROJ_COLS = 512
OUT_ROWS = 512


def _params(*sem):
    return pltpu.CompilerParams(dimension_semantics=sem, vmem_limit_bytes=VMEM_LIMIT)


def _dot_nt(a, b):
    return lax.dot_general(a, b, (((1,), (1,)), ((), ())), preferred_element_type=F32)


def _dot(a, b):
    return jnp.dot(a, b, preferred_element_type=F32)


def _silu(g):
    return g * (1.0 / (1.0 + jnp.exp(-g)))


def _softplus(z):
    return jnp.maximum(z, 0.0) + jnp.log(1.0 + jnp.exp2(jnp.abs(z) * -LOG2E))


def _lam_init(layer):
    return 0.8 - 0.6 * math.exp(-0.3 * layer)


def _inproj_body(x_ref, g_ref, w_ref, q_ref, k_ref, v_ref, gate_ref, h_ref, *, n_sub, q_scale):
    j = pl.program_id(1)

    @pl.when(j == 0)
    def _():
        x = x_ref[...]
        ms = jnp.mean(x * x, axis=-1, keepdims=True)
        h_ref[...] = (x * lax.rsqrt(ms + EPS) * g_ref[...]).astype(BF16)

    def project():
        return _dot(h_ref[...], w_ref[...])

    @pl.when(j < n_sub)
    def _():
        q_ref[...] = (project() * q_scale).astype(q_ref.dtype)

    @pl.when((j >= n_sub) & (j < 2 * n_sub))
    def _():
        k_ref[...] = project()

    @pl.when((j >= 2 * n_sub) & (j < 3 * n_sub))
    def _():
        v_ref[...] = project()

    @pl.when(j >= 3 * n_sub)
    def _():
        gate_ref[...] = project().astype(gate_ref.dtype)


def _inproj(x, g, w, *, q_scale, gate_dtype):
    R, D = x.shape
    BR = w.shape[1] // 4
    tm = min(PROJ_ROWS, R)
    tn = PROJ_COLS
    assert R % tm == 0 and BR % tn == 0
    n_sub = BR // tn

    def out_map(group):
        return lambda i, j: (i, jnp.clip(j - group * n_sub, 0, n_sub - 1))

    return pl.pallas_call(
        functools.partial(_inproj_body, n_sub=n_sub, q_scale=q_scale),
        grid=(R // tm, 4 * n_sub),
        in_specs=[
            pl.BlockSpec((tm, D), lambda i, j: (i, 0)),
            pl.BlockSpec((1, D), lambda i, j: (0, 0)),
            pl.BlockSpec((D, tn), lambda i, j: (0, j)),
        ],
        out_specs=[
            pl.BlockSpec((tm, tn), out_map(0)),
            pl.BlockSpec((tm, tn), out_map(1)),
            pl.BlockSpec((tm, tn), out_map(2)),
            pl.BlockSpec((tm, tn), out_map(3)),
        ],
        out_shape=[
            jax.ShapeDtypeStruct((R, BR), BF16),
            jax.ShapeDtypeStruct((R, BR), F32),
            jax.ShapeDtypeStruct((R, BR), F32),
            jax.ShapeDtypeStruct((R, BR), gate_dtype),
        ],
        scratch_shapes=[pltpu.VMEM((tm, D), BF16)],
        compiler_params=_params("arbitrary", "arbitrary"),
        name="inproj",
    )(x, g.reshape(1, D), w)


def _outproj_body(o_ref, w_ref, g_ref, x_ref, y_ref):
    o = _dot(o_ref[...].astype(BF16), w_ref[...])
    ms = jnp.mean(o * o, axis=-1, keepdims=True)
    y_ref[...] = x_ref[...] + o * lax.rsqrt(ms + EPS) * g_ref[...]


def _outproj(o, w, g, x):
    R, BR = o.shape
    D = w.shape[1]
    tm = min(OUT_ROWS, R)
    assert R % tm == 0
    return pl.pallas_call(
        _outproj_body,
        grid=(R // tm,),
        in_specs=[
            pl.BlockSpec((tm, BR), lambda i: (i, 0)),
            pl.BlockSpec((BR, D), lambda i: (0, 0)),
            pl.BlockSpec((1, D), lambda i: (0, 0)),
            pl.BlockSpec((tm, D), lambda i: (i, 0)),
        ],
        out_specs=pl.BlockSpec((tm, D), lambda i: (i, 0)),
        out_shape=jax.ShapeDtypeStruct((R, D), F32),
        compiler_params=_params("arbitrary"),
        name="outproj",
    )(o, w, g.reshape(1, D), x)


def _bias_body(rb_ref, near_ref, last_ref, new_ref, *, page, dec_seq):
    h = pl.program_id(0)

    def tile(shape, q_of_row, offset):
        row = lax.broadcasted_iota(jnp.int32, shape, 0)
        col = lax.broadcasted_iota(jnp.int32, shape, 1)
        d = q_of_row(row) + offset - col
        n = jnp.maximum(d, 0)
        nf = jnp.maximum(n, 1).astype(F32)
        large = MAX_EXACT + (jnp.log(nf / MAX_EXACT) / math.log(MAX_DISTANCE / MAX_EXACT)
                             * (N_BUCKETS - MAX_EXACT)).astype(jnp.int32)
        large = jnp.minimum(large, N_BUCKETS - 1)
        bucket = jnp.where(n < MAX_EXACT, n, large)
        out = jnp.zeros(shape, F32)
        for b in range(N_BUCKETS):
            out = jnp.where(bucket == b, rb_ref[h, b], out)
        out = (out - rb_ref[h, N_BUCKETS - 1]) * LOG2E
        return jnp.where(d >= 0, out, NEG)

    tq = near_ref.shape[0]
    near_ref[...] = tile((tq, 2 * tq), lambda r: r, tq)
    rows = last_ref.shape[0]
    last_ref[...] = tile((rows, page), lambda r: r % dec_seq, page)
    new_ref[...] = tile((rows, page), lambda r: r % dec_seq, 0)


def _bias_tiles(rel_bias, *, tq, page, dec_seq):
    H = rel_bias.shape[1]
    rows = 2 * dec_seq
    return pl.pallas_call(
        functools.partial(_bias_body, page=page, dec_seq=dec_seq),
        grid=(H,),
        in_specs=[pl.BlockSpec(memory_space=pltpu.SMEM)],
        out_specs=[
            pl.BlockSpec((None, tq, 2 * tq), lambda h: (h, 0, 0)),
            pl.BlockSpec((None, rows, page), lambda h: (h, 0, 0)),
            pl.BlockSpec((None, rows, page), lambda h: (h, 0, 0)),
        ],
        out_shape=[
            jax.ShapeDtypeStruct((H, tq, 2 * tq), F32),
            jax.ShapeDtypeStruct((H, rows, page), F32),
            jax.ShapeDtypeStruct((H, rows, page), F32),
        ],
        compiler_params=_params("arbitrary"),
        name="bias_tiles",
    )(rel_bias.T)


def _lambda(lq1, lk1, lq2, lk2, lam_init):
    s1 = jnp.sum(lq1[...] * lk1[...], axis=-1, keepdims=True)
    s2 = jnp.sum(lq2[...] * lk2[...], axis=-1, keepdims=True)
    return jnp.exp(s1) - jnp.exp(s2) + lam_init


def _diff_epilogue(o, gate, gsub, lam_init):
    ms = jnp.mean(o * o, axis=-1, keepdims=True)
    on = (o * lax.rsqrt(ms + EPS) * gsub) * (1.0 - lam_init)
    return on * _silu(gate.astype(F32))


def _per_query_tile(qi, n_tiles, fn):
    for j in range(n_tiles):
        pl.when(qi == j)(functools.partial(fn, j))


def _diff_prompt_body(q_ref, k_ref, v_ref, gate_ref, near_ref,
                      lq1, lk1, lq2, lk2, gsub_ref, o_ref, kb_ref, vb_ref, *, lam_init):
    qi = pl.program_id(2)
    tq = q_ref.shape[0]
    d = HEAD_DIM_A

    @pl.when(qi == 0)
    def _():
        kb_ref[...] = k_ref[...].astype(BF16)
        vb_ref[...] = v_ref[...].astype(BF16)

    def softmax_map(j, hh, mp):
        nk = (j + 1) * tq
        nn = min(nk, 2 * tq)
        nf = nk - nn
        cols = slice(hh * VDIM_A + mp * d, hh * VDIM_A + (mp + 1) * d)
        vcols = slice(hh * VDIM_A, (hh + 1) * VDIM_A)
        qm = q_ref[:, cols]
        sn = _dot_nt(qm, kb_ref[nf:nk, cols]) + near_ref[hh, :, 2 * tq - nn:]
        m = jnp.max(sn, axis=-1, keepdims=True)
        if nf:
            sf = _dot_nt(qm, kb_ref[0:nf, cols])
            m = jnp.maximum(m, jnp.max(sf, axis=-1, keepdims=True))
        pn = jnp.exp2(sn - m)
        l = jnp.sum(pn, axis=-1, keepdims=True)
        o = _dot(pn.astype(BF16), vb_ref[nf:nk, vcols])
        if nf:
            pf = jnp.exp2(sf - m)
            l = l + jnp.sum(pf, axis=-1, keepdims=True)
            o = o + _dot(pf.astype(BF16), vb_ref[0:nf, vcols])
        return o / l

    def tile(j):
        lam = _lambda(lq1, lk1, lq2, lk2, lam_init)
        for hh in range(q_ref.shape[1] // VDIM_A):
            vcols = slice(hh * VDIM_A, (hh + 1) * VDIM_A)
            o = softmax_map(j, hh, 0) - lam * softmax_map(j, hh, 1)
            o_ref[:, vcols] = _diff_epilogue(
                o, gate_ref[:, vcols], gsub_ref[...], lam_init).astype(o_ref.dtype)

    _per_query_tile(qi, pl.num_programs(2), tile)


def _diff_prompt(q, k, v, gate, near, lams, gsub, *, lam_init):
    B, T, BR = q.shape
    tq = Q_TILE
    assert T % tq == 0
    PAIR = 2
    H, W = N_HEADS_A // PAIR, PAIR * VDIM_A
    vec = pl.BlockSpec((1, HEAD_DIM_A), lambda b, h, i: (0, 0))
    return pl.pallas_call(
        functools.partial(_diff_prompt_body, lam_init=lam_init),
        grid=(B, H, T // tq),
        in_specs=[
            pl.BlockSpec((None, tq, W), lambda b, h, i: (b, i, h)),
            pl.BlockSpec((None, T, W), lambda b, h, i: (b, 0, h)),
            pl.BlockSpec((None, T, W), lambda b, h, i: (b, 0, h)),
            pl.BlockSpec((None, tq, W), lambda b, h, i: (b, i, h)),
            pl.BlockSpec((PAIR, tq, 2 * tq), lambda b, h, i: (h, 0, 0)),
            vec, vec, vec, vec,
            pl.BlockSpec((1, VDIM_A), lambda b, h, i: (0, 0)),
        ],
        out_specs=pl.BlockSpec((None, tq, W), lambda b, h, i: (b, i, h)),
        out_shape=jax.ShapeDtypeStruct((B, T, BR), BF16),
        scratch_shapes=[
            pltpu.VMEM((T, W), BF16),
            pltpu.VMEM((T, W), BF16),
        ],
        compiler_params=_params("arbitrary", "arbitrary", "arbitrary"),
        name="diff_prompt",
    )(q, k, v, gate, near, *lams, gsub)


def _suffix_sum(sp, tri):
    return _dot(sp.astype(BF16), tri)


def _sb_prompt_body(q_ref, k_ref, v_ref, gate_ref, tri_ref, o_ref, kb_ref, vb_ref):
    qi = pl.program_id(2)
    tq = q_ref.shape[0]

    @pl.when(qi == 0)
    def _():
        kb_ref[...] = k_ref[...].astype(BF16)
        vb_ref[...] = v_ref[...].astype(BF16)

    def head(j, cols):
        nk = (j + 1) * tq
        tri = tri_ref[...]
        row = lax.broadcasted_iota(jnp.int32, (tq, tq), 0)
        col = lax.broadcasted_iota(jnp.int32, (tq, tq), 1)
        mask = col < row
        z_all = _dot_nt(q_ref[:, cols], kb_ref[0:nk, cols])
        carry = None
        a_tiles = []
        for c in range(j, -1, -1):
            z = z_all[:, c * tq:(c + 1) * tq]
            sp = _softplus(z)
            if c == j:
                sp = jnp.where(mask, sp, 0.0)
            log_a = z - _suffix_sum(sp, tri)
            if carry is not None:
                log_a = log_a - carry
            a = jnp.exp(log_a)
            if c == j:
                a = jnp.where(mask, a, 0.0)
            a_tiles.append(a.astype(BF16))
            if c:
                tot = jnp.sum(sp, axis=-1, keepdims=True)
                carry = tot if carry is None else carry + tot
        acc = _dot(jnp.concatenate(a_tiles[::-1], axis=1), vb_ref[0:nk, cols])
        o_ref[:, cols] = (acc * _silu(gate_ref[:, cols].astype(F32))).astype(o_ref.dtype)

    def tile(j):
        for hh in range(q_ref.shape[1] // HEAD_DIM_B):
            head(j, slice(hh * HEAD_DIM_B, (hh + 1) * HEAD_DIM_B))

    _per_query_tile(qi, pl.num_programs(2), tile)


def _tri(n):
    return jnp.tril(jnp.ones((n, n), BF16))


def _sb_prompt(q, k, v, gate):
    B, T, BR = q.shape
    tq = Q_TILE
    assert T % tq == 0
    GROUP = 2
    H, W = N_HEADS_B // GROUP, GROUP * HEAD_DIM_B
    return pl.pallas_call(
        _sb_prompt_body,
        grid=(B, H, T // tq),
        in_specs=[
            pl.BlockSpec((None, tq, W), lambda b, h, i: (b, i, h)),
            pl.BlockSpec((None, T, W), lambda b, h, i: (b, 0, h)),
            pl.BlockSpec((None, T, W), lambda b, h, i: (b, 0, h)),
            pl.BlockSpec((None, tq, W), lambda b, h, i: (b, i, h)),
            pl.BlockSpec((tq, tq), lambda b, h, i: (0, 0)),
        ],
        out_specs=pl.BlockSpec((None, tq, W), lambda b, h, i: (b, i, h)),
        out_shape=jax.ShapeDtypeStruct((B, T, BR), BF16),
        scratch_shapes=[
            pltpu.VMEM((T, W), BF16),
            pltpu.VMEM((T, W), BF16),
        ],
        compiler_params=_params("arbitrary", "arbitrary", "arbitrary"),
        name="sb_prompt",
    )(q, k, v, gate, _tri(tq))


def _block_diag_q(q, dec_seq):
    DB = q.shape[0]
    t = q.reshape(DB, dec_seq, N_GROUPS, LANE).transpose(0, 2, 1, 3)
    eye = jnp.eye(N_GROUPS, dtype=q.dtype)
    bd = t[:, :, :, None, :] * eye[None, :, None, :, None]
    return bd.reshape(DB, N_GROUPS * dec_seq, N_GROUPS * LANE)


def _half_page_specs(layer, pages_of_step, page, lane_halves):
    def spec(s, half):
        def index(b, g, pt):
            p = pt[b, pages_of_step(g, s)]
            return (layer, p, 0, 0, half) if lane_halves else (layer, p, 0, half, 0)
        return pl.BlockSpec((None, None, page, SUBLANE, LANE), index)
    return [spec(s, half) for s in range(PAGES_PER_STEP) for half in range(2)]


def _regroup(dst_ref, slot, halves, column_of):
    page = halves[0].shape[0]
    for half, ref in enumerate(halves):
        rows = ref.reshape(page * SUBLANE, LANE)
        for r in range(SUBLANE):
            c = column_of(half, r)
            dst_ref[slot * page:(slot + 1) * page, c * LANE:(c + 1) * LANE] = (
                rows[pl.ds(r, page, stride=SUBLANE), :].astype(BF16))


def _new_page(new_ref, page):
    n = new_ref.shape[0]
    pad = jnp.zeros((page - n, new_ref.shape[1]), F32)
    return jnp.concatenate([new_ref[...], pad], axis=0).astype(BF16)


def _diff_sample_body(pt_ref, q_ref, *refs, lam_init, n_steps):
    P = PAGES_PER_STEP
    k_refs, v_refs = refs[:2 * P], refs[2 * P:4 * P]
    (knew_ref, vnew_ref, last_ref, new_ref, gate_ref, lq1, lk1, lq2, lk2, gsub_ref,
     o_ref, kp_ref, vp_ref, m_ref, l_ref, acc_ref, r_ref) = refs[4 * P:]
    g = pl.program_id(1)
    W = VDIM_A
    rows_h = acc_ref.shape[0] // N_HEADS_A
    dec_seq = rows_h // 2

    @pl.when((pl.program_id(0) == 0) & (g == 0))
    def _():
        kp_ref[1] = jnp.zeros(kp_ref.shape[1:], BF16)
        vp_ref[1] = jnp.zeros(vp_ref.shape[1:], BF16)

    @pl.when(g == 0)
    def _():
        m_ref[...] = jnp.full(m_ref.shape, NEG, F32)
        l_ref[...] = jnp.zeros(l_ref.shape, F32)
        acc_ref[...] = jnp.zeros(acc_ref.shape, F32)

    def regroup(half):
        for s in range(PAGES_PER_HALF):
            slot = half * PAGES_PER_HALF + s
            _regroup(kp_ref.at[half], s, k_refs[2 * slot:2 * slot + 2], lambda hf, r: hf * SUBLANE + r)
            _regroup(vp_ref.at[half], s, v_refs[2 * slot:2 * slot + 2], lambda hf, r: 2 * r + hf)

    def update(kpage, vpage, bias):
        s = _dot_nt(q_ref[...], kpage)
        if bias is not None:
            s = s + bias
        m_old = m_ref[...]
        m_new = jnp.maximum(m_old, jnp.max(s, axis=-1, keepdims=True))
        p = jnp.exp2(s - m_new)
        alpha = jnp.exp2(m_old - m_new)
        l_ref[...] = alpha * l_ref[...] + jnp.sum(p, axis=-1, keepdims=True)
        m_ref[...] = m_new
        pv = _dot(p.astype(BF16), vpage)
        for h in range(N_HEADS_A):
            r0 = h * rows_h
            acc_ref[r0:r0 + rows_h, :] = (alpha[r0:r0 + rows_h] * acc_ref[r0:r0 + rows_h, :]
                                          + pv[r0:r0 + rows_h, h * W:(h + 1) * W])

    update(kp_ref[1], vp_ref[1], jnp.where(g == 0, NEG, 0.0))
    regroup(0)
    regroup(1)
    update(kp_ref[0], vp_ref[0], None)

    @pl.when(g == n_steps - 1)
    def _():
        page = k_refs[0].shape[0]
        update(kp_ref[1], vp_ref[1], last_ref[...])
        update(_new_page(knew_ref, page), _new_page(vnew_ref, page), new_ref[...])
        r_ref[...] = acc_ref[...] / l_ref[...]
        lam = _lambda(lq1, lk1, lq2, lk2, lam_init)
        for h in range(N_HEADS_A):
            r0 = h * rows_h
            o = r_ref[r0:r0 + dec_seq, :] - lam * r_ref[r0 + dec_seq:r0 + rows_h, :]
            o_ref[:, h * W:(h + 1) * W] = _diff_epilogue(
                o, gate_ref[:, h * W:(h + 1) * W], gsub_ref[...], lam_init)


def _diff_sample(qbd, cache_k, cache_v, layer, page_table, knew, vnew, last, new, gate, lams, gsub,
                 *, lam_init):
    DB, rows, BR = qbd.shape
    dec_seq = knew.shape[1]
    page = cache_k.shape[2]
    n_pages = page_table.shape[1]
    P = PAGES_PER_STEP
    assert n_pages % P == 0
    n_steps = n_pages // P
    full = lambda shape: pl.BlockSpec(shape, lambda b, g, pt: (0,) * len(shape))
    per_seq = lambda r: pl.BlockSpec((None, r, BR), lambda b, g, pt: (b, 0, 0))
    forward = lambda g, s: g * P + s
    half_keys = PAGES_PER_HALF * page
    last = jnp.pad(last, ((0, 0), (half_keys - page, 0)))
    grid_spec = pltpu.PrefetchScalarGridSpec(
        num_scalar_prefetch=1,
        grid=(DB, n_steps),
        in_specs=[per_seq(rows)]
        + _half_page_specs(layer, forward, page, False) + _half_page_specs(layer, forward, page, True)
        + [per_seq(dec_seq), per_seq(dec_seq), full((rows, half_keys)), full((rows, page)), per_seq(dec_seq)]
        + [full((1, HEAD_DIM_A))] * 4 + [full((1, VDIM_A))],
        out_specs=per_seq(dec_seq),
        scratch_shapes=[
            pltpu.VMEM((2, half_keys, BR), BF16),
            pltpu.VMEM((2, half_keys, BR), BF16),
            pltpu.VMEM((rows, 1), F32),
            pltpu.VMEM((rows, 1), F32),
            pltpu.VMEM((rows, VDIM_A), F32),
            pltpu.VMEM((rows, VDIM_A), F32),
        ],
    )
    return pl.pallas_call(
        functools.partial(_diff_sample_body, lam_init=lam_init, n_steps=n_steps),
        grid_spec=grid_spec,
        out_shape=jax.ShapeDtypeStruct((DB, dec_seq, BR), F32),
        compiler_params=_params("arbitrary", "arbitrary"),
        name="diff_sample",
    )(page_table, qbd, *([cache_k] * (2 * P)), *([cache_v] * (2 * P)),
      knew, vnew, last, new, gate, *lams, gsub)


def _sb_sample_body(pt_ref, q_ref, *refs, n_steps):
    P = PAGES_PER_STEP
    k_refs, v_refs = refs[:2 * P], refs[2 * P:4 * P]
    (knew_ref, vnew_ref, mask_ref, gate_ref, tri_step_ref, tri_page_ref, o_ref,
     kp_ref, vp_ref, carry_ref, acc_ref) = refs[4 * P:]
    g = pl.program_id(1)
    W = HEAD_DIM_B
    dec_seq = o_ref.shape[0]
    n_pairs = N_HEADS_B // 2
    rows_p = 2 * dec_seq

    def update(kpage, vpage, mask, tri_ref, stale=None):
        z = _dot_nt(q_ref[...], kpage)
        sp = _softplus(z)
        if mask is not None:
            sp = sp * mask
        cum = _suffix_sum(sp, tri_ref[...])
        log_a = z - cum - carry_ref[...]
        if stale is not None:
            log_a = log_a + jnp.where(stale, NEG, 0.0)
        a = jnp.exp(log_a)
        if mask is not None:
            a = a * mask
        pv = _dot(a.astype(BF16), vpage)
        for hp in range(n_pairs):
            r0 = hp * rows_p
            acc_ref[r0:r0 + rows_p, :] += pv[r0:r0 + rows_p, hp * 2 * W:(hp + 1) * 2 * W]
        tot = cum[:, 0:1]
        if stale is not None:
            tot = tot * jnp.where(stale, 0.0, 1.0)
        carry_ref[...] += tot

    @pl.when((pl.program_id(0) == 0) & (g == 0))
    def _():
        kp_ref[1] = jnp.zeros(kp_ref.shape[1:], BF16)
        vp_ref[1] = jnp.zeros(vp_ref.shape[1:], BF16)

    @pl.when(g == 0)
    def _():
        carry_ref[...] = jnp.zeros(carry_ref.shape, F32)
        acc_ref[...] = jnp.zeros(acc_ref.shape, F32)
        page = k_refs[0].shape[0]
        update(_new_page(knew_ref, page), _new_page(vnew_ref, page), mask_ref[...], tri_page_ref)

    def regroup(half):
        for s in range(PAGES_PER_HALF):
            slot = half * PAGES_PER_HALF + s
            by_head = lambda hf, r: hf * SUBLANE + r
            _regroup(kp_ref.at[half], PAGES_PER_HALF - 1 - s, k_refs[2 * slot:2 * slot + 2], by_head)
            _regroup(vp_ref.at[half], PAGES_PER_HALF - 1 - s, v_refs[2 * slot:2 * slot + 2], by_head)

    update(kp_ref[1], vp_ref[1], None, tri_step_ref, stale=g == 0)
    regroup(0)
    regroup(1)
    update(kp_ref[0], vp_ref[0], None, tri_step_ref)

    @pl.when(g == n_steps - 1)
    def _():
        update(kp_ref[1], vp_ref[1], None, tri_step_ref)
        for hp in range(n_pairs):
            r0 = hp * rows_p
            for e in range(2):
                h = 2 * hp + e
                o = acc_ref[r0 + e * dec_seq:r0 + (e + 1) * dec_seq, e * W:(e + 1) * W]
                o_ref[:, h * W:(h + 1) * W] = o * _silu(gate_ref[:, h * W:(h + 1) * W].astype(F32))


def _sb_sample(qbd, cache_k, cache_v, layer, page_table, knew, vnew, gate):
    DB, rows, BR = qbd.shape
    dec_seq = knew.shape[1]
    page = cache_k.shape[2]
    n_pages = page_table.shape[1]
    P = PAGES_PER_STEP
    assert n_pages % P == 0
    n_steps = n_pages // P
    half_keys = PAGES_PER_HALF * page
    t_of_row = jnp.arange(rows, dtype=jnp.int32)[:, None] % dec_seq
    mask = (jnp.arange(page, dtype=jnp.int32)[None, :] < t_of_row).astype(F32)
    full = lambda shape: pl.BlockSpec(shape, lambda b, g, pt: (0,) * len(shape))
    per_seq = lambda r: pl.BlockSpec((None, r, BR), lambda b, g, pt: (b, 0, 0))
    backward = lambda g, s: n_pages - 1 - (g * P + s)
    grid_spec = pltpu.PrefetchScalarGridSpec(
        num_scalar_prefetch=1,
        grid=(DB, n_steps),
        in_specs=[per_seq(rows)]
        + _half_page_specs(layer, backward, page, False) + _half_page_specs(layer, backward, page, False)
        + [per_seq(dec_seq), per_seq(dec_seq), full((rows, page)), per_seq(dec_seq),
           full((half_keys, half_keys)), full((page, page))],
        out_specs=per_seq(dec_seq),
        scratch_shapes=[
            pltpu.VMEM((2, half_keys, BR), BF16),
            pltpu.VMEM((2, half_keys, BR), BF16),
            pltpu.VMEM((rows, 1), F32),
            pltpu.VMEM((rows, 2 * HEAD_DIM_B), F32),
        ],
    )
    return pl.pallas_call(
        functools.partial(_sb_sample_body, n_steps=n_steps),
        grid_spec=grid_spec,
        out_shape=jax.ShapeDtypeStruct((DB, dec_seq, BR), F32),
        compiler_params=_params("arbitrary", "arbitrary"),
        name="sb_sample",
    )(page_table, qbd, *([cache_k] * (2 * P)), *([cache_v] * (2 * P)), knew, vnew, mask, gate,
      _tri(half_keys), _tri(page))


def _trunk(x, caches, page_table, tiles, norm_pre, norm_post, w_in, w_out, lams, subln):
    B, T, D = x.shape
    depth = norm_pre.shape[0]
    near, last, new = tiles
    cache_k_diff, cache_v_diff, cache_k_sb, cache_v_sb = caches
    sample = page_table is not None
    xf = x.reshape(B * T, D)
    kd, vd, ks, vs = [], [], [], []
    for i in range(depth):
        j = i // 2
        diff = i % 2 == 0
        q_scale = HEAD_DIM_A ** -0.5 * LOG2E if diff else HEAD_DIM_B ** -0.5
        q, k, v, gate = _inproj(xf, norm_pre[i], w_in[i], q_scale=q_scale,
                                gate_dtype=F32 if sample else BF16)
        BR = q.shape[1]
        shp = lambda a: a.reshape(B, T, BR)
        if diff:
            lam_init = _lam_init(i)
            lam_j = tuple(l[j].reshape(1, HEAD_DIM_A) for l in lams)
            gsub = subln[j].reshape(1, VDIM_A)
            if sample:
                o = _diff_sample(_block_diag_q(shp(q), T), cache_k_diff, cache_v_diff, j, page_table,
                                 shp(k), shp(v), last, new, shp(gate), lam_j, gsub, lam_init=lam_init)
            else:
                o = _diff_prompt(shp(q), shp(k), shp(v), shp(gate), near, lam_j, gsub,
                                 lam_init=lam_init)
            kd.append(k.reshape(B, T, N_HEADS_A, 2, HEAD_DIM_A))
            vd.append(v.reshape(B, T, N_HEADS_A, VDIM_A))
        else:
            if sample:
                o = _sb_sample(_block_diag_q(shp(q), T), cache_k_sb, cache_v_sb, j, page_table,
                               shp(k), shp(v), shp(gate))
            else:
                o = _sb_prompt(shp(q), shp(k), shp(v), shp(gate))
            ks.append(k.reshape(B, T, N_HEADS_B, HEAD_DIM_B))
            vs.append(v.reshape(B, T, N_HEADS_B, HEAD_DIM_B))
        xf = _outproj(o.reshape(B * T, BR), w_out[i], norm_post[i], xf)
    return xf.reshape(B, T, D), jnp.stack(kd), jnp.stack(vd), jnp.stack(ks), jnp.stack(vs)


def kernel(x_prompt, x_sample, cache_k_diff, cache_v_diff, cache_k_sb, cache_v_sb, page_table, rel_bias, norm_pre, norm_post, w_in_diff, w_out_diff, lambda_q1, lambda_k1, lambda_q2, lambda_k2, subln_diff, w_in_sb, w_out_sb):
    depth = norm_pre.shape[0]
    page = cache_k_diff.shape[2]
    dec_seq = x_sample.shape[1]
    w_in = [(w_in_diff if i % 2 == 0 else w_in_sb)[i // 2].astype(BF16) for i in range(depth)]
    w_out = [(w_out_diff if i % 2 == 0 else w_out_sb)[i // 2].astype(BF16) for i in range(depth)]
    kd_shape = cache_k_diff.shape
    caches = (cache_k_diff.reshape(kd_shape[:3] + (N_GROUPS, LANE)), cache_v_diff, cache_k_sb, cache_v_sb)
    near, last, new = _bias_tiles(rel_bias, tq=Q_TILE, page=page, dec_seq=dec_seq)
    rows = N_GROUPS * dec_seq
    tiles = (near, last.reshape(rows, page), new.reshape(rows, page))
    lams = (lambda_q1, lambda_k1, lambda_q2, lambda_k2)
    args = (tiles, norm_pre, norm_post, w_in, w_out, lams, subln_diff)
    y_p, kd_p, vd_p, ks_p, vs_p = _trunk(x_prompt, caches, None, *args)
    y_s, kd_s, vd_s, ks_s, vs_s = _trunk(x_sample, caches, page_table, *args)
    return (y_p, y_s, kd_p, vd_p, ks_p, vs_p, kd_s, vd_s, ks_s, vs_s)
```

```python
import functools
import math

import jax
import jax.numpy as jnp
from jax import lax
from jax.experimental import pallas as pl
from jax.experimental.pallas import tpu as pltpu

F32 = jnp.float32
BF16 = jnp.bfloat16

N_HEADS_A = 8
HEAD_DIM_A = 128
VDIM_A = 2 * HEAD_DIM_A
N_HEADS_B = 16
HEAD_DIM_B = 128
N_GROUPS = 16
N_BUCKETS = 32
MAX_EXACT = N_BUCKETS // 2
MAX_DISTANCE = 128
EPS = 1e-6
NEG = -1e30
LOG2E = math.log2(math.e)

LANE = 128
SUBLANE = 8
VMEM_LIMIT = 56 * 1024 * 1024

Q_TILE = 256
PAGES_PER_STEP = 8
PAGES_PER_HALF = PAGES_PER_STEP // 2
PROJ_ROWS = 512
PROJ_COLS = SUBLANE * LANE
OUT_ROWS = 512


def _params(*sem):
    return pltpu.CompilerParams(dimension_semantics=sem, vmem_limit_bytes=VMEM_LIMIT)


def _dot_nt(a, b):
    return lax.dot_general(a, b, (((1,), (1,)), ((), ())), preferred_element_type=F32)


def _dot(a, b):
    return jnp.dot(a, b, preferred_element_type=F32)


def _silu(g):
    return g * (1.0 / (1.0 + jnp.exp(-g)))


def _softplus(z):
    return jnp.maximum(z, 0.0) + jnp.log(1.0 + jnp.exp2(jnp.abs(z) * -LOG2E))


def _lam_init(layer):
    return 0.8 - 0.6 * math.exp(-0.3 * layer)


def _inproj_body(x_ref, g_ref, w_ref, q_ref, k_ref, kb_ref, v_ref, gate_ref, h_ref, *, n_sub, q_scale):
    j = pl.program_id(1)
    tm = x_ref.shape[0]

    @pl.when(j == 0)
    def _():
        x = x_ref[...]
        ms = jnp.mean(x * x, axis=-1, keepdims=True)
        h_ref[...] = (x * lax.rsqrt(ms + EPS) * g_ref[...]).astype(BF16)

    def project():
        return _dot(h_ref[...], w_ref[...])

    @pl.when(j < n_sub)
    def _():
        q_ref[...] = (project() * q_scale).astype(q_ref.dtype)

    @pl.when((j >= n_sub) & (j < 2 * n_sub))
    def _():
        k = project()
        kb_ref[...] = k.astype(BF16)
        rows = k_ref.reshape(tm * SUBLANE, LANE)
        for r in range(SUBLANE):
            rows[pl.ds(r, tm, stride=SUBLANE), :] = k[:, r * LANE:(r + 1) * LANE]

    @pl.when((j >= 2 * n_sub) & (j < 3 * n_sub))
    def _():
        v_ref[...] = project()

    @pl.when(j >= 3 * n_sub)
    def _():
        gate_ref[...] = project().astype(gate_ref.dtype)


def _inproj(x, g, w, *, q_scale, gate_dtype):
    R, D = x.shape
    BR = w.shape[1] // 4
    tm = min(PROJ_ROWS, R)
    tn = PROJ_COLS
    assert R % tm == 0 and BR % tn == 0 and tn == SUBLANE * LANE
    n_sub = BR // tn

    def out_map(group):
        return lambda i, j: (i, jnp.clip(j - group * n_sub, 0, n_sub - 1))

    def k_map(i, j):
        return (i, jnp.clip(j - n_sub, 0, n_sub - 1), 0)

    return pl.pallas_call(
        functools.partial(_inproj_body, n_sub=n_sub, q_scale=q_scale),
        grid=(R // tm, 4 * n_sub),
        in_specs=[
            pl.BlockSpec((tm, D), lambda i, j: (i, 0)),
            pl.BlockSpec((1, D), lambda i, j: (0, 0)),
            pl.BlockSpec((D, tn), lambda i, j: (0, j)),
        ],
        out_specs=[
            pl.BlockSpec((tm, tn), out_map(0)),
            pl.BlockSpec((tm, SUBLANE, LANE), k_map),
            pl.BlockSpec((tm, tn), out_map(1)),
            pl.BlockSpec((tm, tn), out_map(2)),
            pl.BlockSpec((tm, tn), out_map(3)),
        ],
        out_shape=[
            jax.ShapeDtypeStruct((R, BR), BF16),
            jax.ShapeDtypeStruct((R, BR // LANE, LANE), F32),
            jax.ShapeDtypeStruct((R, BR), BF16),
            jax.ShapeDtypeStruct((R, BR), F32),
            jax.ShapeDtypeStruct((R, BR), gate_dtype),
        ],
        scratch_shapes=[pltpu.VMEM((tm, D), BF16)],
        compiler_params=_params("arbitrary", "arbitrary"),
        name="inproj",
    )(x, g.reshape(1, D), w)


def _outproj_body(o_ref, w_ref, g_ref, x_ref, y_ref):
    o = _dot(o_ref[...].astype(BF16), w_ref[...])
    ms = jnp.mean(o * o, axis=-1, keepdims=True)
    y_ref[...] = x_ref[...] + o * lax.rsqrt(ms + EPS) * g_ref[...]


def _outproj(o, w, g, x):
    R, BR = o.shape
    D = w.shape[1]
    tm = min(OUT_ROWS, R)
    assert R % tm == 0
    return pl.pallas_call(
        _outproj_body,
        grid=(R // tm,),
        in_specs=[
            pl.BlockSpec((tm, BR), lambda i: (i, 0)),
            pl.BlockSpec((BR, D), lambda i: (0, 0)),
            pl.BlockSpec((1, D), lambda i: (0, 0)),
            pl.BlockSpec((tm, D), lambda i: (i, 0)),
        ],
        out_specs=pl.BlockSpec((tm, D), lambda i: (i, 0)),
        out_shape=jax.ShapeDtypeStruct((R, D), F32),
        compiler_params=_params("arbitrary"),
        name="outproj",
    )(o, w, g.reshape(1, D), x)


def _bias_body(rb_ref, near_ref, last_ref, new_ref, *, page, dec_seq):
    h = pl.program_id(0)

    def tile(shape, q_of_row, offset):
        row = lax.broadcasted_iota(jnp.int32, shape, 0)
        col = lax.broadcasted_iota(jnp.int32, shape, 1)
        d = q_of_row(row) + offset - col
        n = jnp.maximum(d, 0)
        nf = jnp.maximum(n, 1).astype(F32)
        large = MAX_EXACT + (jnp.log(nf / MAX_EXACT) / math.log(MAX_DISTANCE / MAX_EXACT)
                             * (N_BUCKETS - MAX_EXACT)).astype(jnp.int32)
        large = jnp.minimum(large, N_BUCKETS - 1)
        bucket = jnp.where(n < MAX_EXACT, n, large)
        out = jnp.zeros(shape, F32)
        for b in range(N_BUCKETS):
            out = jnp.where(bucket == b, rb_ref[h, b], out)
        out = (out - rb_ref[h, N_BUCKETS - 1]) * LOG2E
        return jnp.where(d >= 0, out, NEG)

    tq = near_ref.shape[0]
    near_ref[...] = tile((tq, 2 * tq), lambda r: r, tq)
    rows = last_ref.shape[0]
    last_ref[...] = tile((rows, page), lambda r: r % dec_seq, page)
    new_ref[...] = tile((rows, page), lambda r: r % dec_seq, 0)


def _bias_tiles(rel_bias, *, tq, page, dec_seq):
    H = rel_bias.shape[1]
    rows = 2 * dec_seq
    return pl.pallas_call(
        functools.partial(_bias_body, page=page, dec_seq=dec_seq),
        grid=(H,),
        in_specs=[pl.BlockSpec(memory_space=pltpu.SMEM)],
        out_specs=[
            pl.BlockSpec((None, tq, 2 * tq), lambda h: (h, 0, 0)),
            pl.BlockSpec((None, rows, page), lambda h: (h, 0, 0)),
            pl.BlockSpec((None, rows, page), lambda h: (h, 0, 0)),
        ],
        out_shape=[
            jax.ShapeDtypeStruct((H, tq, 2 * tq), F32),
            jax.ShapeDtypeStruct((H, rows, page), F32),
            jax.ShapeDtypeStruct((H, rows, page), F32),
        ],
        compiler_params=_params("arbitrary"),
        name="bias_tiles",
    )(rel_bias.T)


def _lambda(lq1, lk1, lq2, lk2, lam_init):
    s1 = jnp.sum(lq1[...] * lk1[...], axis=-1, keepdims=True)
    s2 = jnp.sum(lq2[...] * lk2[...], axis=-1, keepdims=True)
    return jnp.exp(s1) - jnp.exp(s2) + lam_init


def _diff_epilogue(o, gate, gsub, lam_init):
    ms = jnp.mean(o * o, axis=-1, keepdims=True)
    on = (o * lax.rsqrt(ms + EPS) * gsub) * (1.0 - lam_init)
    return on * _silu(gate.astype(F32))


def _per_query_tile(qi, n_tiles, fn):
    for j in range(n_tiles):
        pl.when(qi == j)(functools.partial(fn, j))


def _diff_prompt_body(q_ref, k_ref, v_ref, gate_ref, near_ref,
                      lq1, lk1, lq2, lk2, gsub_ref, o_ref, vb_ref, *, lam_init):
    qi = pl.program_id(2)
    tq = q_ref.shape[0]
    d = HEAD_DIM_A
    kb_ref = k_ref

    @pl.when(qi == 0)
    def _():
        vb_ref[...] = v_ref[...].astype(BF16)

    def softmax_map(j, hh, mp):
        nk = (j + 1) * tq
        nn = min(nk, 2 * tq)
        nf = nk - nn
        cols = slice(hh * VDIM_A + mp * d, hh * VDIM_A + (mp + 1) * d)
        vcols = slice(hh * VDIM_A, (hh + 1) * VDIM_A)
        qm = q_ref[:, cols]
        sn = _dot_nt(qm, kb_ref[nf:nk, cols]) + near_ref[hh, :, 2 * tq - nn:]
        m = jnp.max(sn, axis=-1, keepdims=True)
        if nf:
            sf = _dot_nt(qm, kb_ref[0:nf, cols])
            m = jnp.maximum(m, jnp.max(sf, axis=-1, keepdims=True))
        pn = jnp.exp2(sn - m)
        l = jnp.sum(pn, axis=-1, keepdims=True)
        o = _dot(pn.astype(BF16), vb_ref[nf:nk, vcols])
        if nf:
            pf = jnp.exp2(sf - m)
            l = l + jnp.sum(pf, axis=-1, keepdims=True)
            o = o + _dot(pf.astype(BF16), vb_ref[0:nf, vcols])
        return o / l

    def tile(j):
        lam = _lambda(lq1, lk1, lq2, lk2, lam_init)
        for hh in range(q_ref.shape[1] // VDIM_A):
            vcols = slice(hh * VDIM_A, (hh + 1) * VDIM_A)
            o = softmax_map(j, hh, 0) - lam * softmax_map(j, hh, 1)
            o_ref[:, vcols] = _diff_epilogue(
                o, gate_ref[:, vcols], gsub_ref[...], lam_init).astype(o_ref.dtype)

    _per_query_tile(qi, pl.num_programs(2), tile)


def _diff_prompt(q, k, v, gate, near, lams, gsub, *, lam_init):
    B, T, BR = q.shape
    tq = Q_TILE
    assert T % tq == 0
    PAIR = 2
    H, W = N_HEADS_A // PAIR, PAIR * VDIM_A
    vec = pl.BlockSpec((1, HEAD_DIM_A), lambda b, h, i: (0, 0))
    return pl.pallas_call(
        functools.partial(_diff_prompt_body, lam_init=lam_init),
        grid=(B, H, T // tq),
        in_specs=[
            pl.BlockSpec((None, tq, W), lambda b, h, i: (b, i, h)),
            pl.BlockSpec((None, T, W), lambda b, h, i: (b, 0, h)),
            pl.BlockSpec((None, T, W), lambda b, h, i: (b, 0, h)),
            pl.BlockSpec((None, tq, W), lambda b, h, i: (b, i, h)),
            pl.BlockSpec((PAIR, tq, 2 * tq), lambda b, h, i: (h, 0, 0)),
            vec, vec, vec, vec,
            pl.BlockSpec((1, VDIM_A), lambda b, h, i: (0, 0)),
        ],
        out_specs=pl.BlockSpec((None, tq, W), lambda b, h, i: (b, i, h)),
        out_shape=jax.ShapeDtypeStruct((B, T, BR), BF16),
        scratch_shapes=[
            pltpu.VMEM((T, W), BF16),
        ],
        compiler_params=_params("arbitrary", "arbitrary", "arbitrary"),
        name="diff_prompt",
    )(q, k, v, gate, near, *lams, gsub)


def _suffix_sum(sp, tri):
    return _dot(sp.astype(BF16), tri)


def _sb_prompt_body(q_ref, k_ref, v_ref, gate_ref, tri_ref, o_ref, vb_ref):
    qi = pl.program_id(2)
    tq = q_ref.shape[0]
    kb_ref = k_ref

    @pl.when(qi == 0)
    def _():
        vb_ref[...] = v_ref[...].astype(BF16)

    def head(j, cols):
        nk = (j + 1) * tq
        tri = tri_ref[...]
        row = lax.broadcasted_iota(jnp.int32, (tq, tq), 0)
        col = lax.broadcasted_iota(jnp.int32, (tq, tq), 1)
        mask = col < row
        z_all = _dot_nt(q_ref[:, cols], kb_ref[0:nk, cols])
        carry = None
        a_tiles = []
        for c in range(j, -1, -1):
            z = z_all[:, c * tq:(c + 1) * tq]
            sp = _softplus(z)
            if c == j:
                sp = jnp.where(mask, sp, 0.0)
            log_a = z - _suffix_sum(sp, tri)
            if carry is not None:
                log_a = log_a - carry
            a = jnp.exp(log_a)
            if c == j:
                a = jnp.where(mask, a, 0.0)
            a_tiles.append(a.astype(BF16))
            if c:
                tot = jnp.sum(sp, axis=-1, keepdims=True)
                carry = tot if carry is None else carry + tot
        acc = _dot(jnp.concatenate(a_tiles[::-1], axis=1), vb_ref[0:nk, cols])
        o_ref[:, cols] = (acc * _silu(gate_ref[:, cols].astype(F32))).astype(o_ref.dtype)

    def tile(j):
        for hh in range(q_ref.shape[1] // HEAD_DIM_B):
            head(j, slice(hh * HEAD_DIM_B, (hh + 1) * HEAD_DIM_B))

    _per_query_tile(qi, pl.num_programs(2), tile)


def _tri(n):
    return jnp.tril(jnp.ones((n, n), BF16))


def _sb_prompt(q, k, v, gate):
    B, T, BR = q.shape
    tq = Q_TILE
    assert T % tq == 0
    GROUP = 2
    H, W = N_HEADS_B // GROUP, GROUP * HEAD_DIM_B
    return pl.pallas_call(
        _sb_prompt_body,
        grid=(B, H, T // tq),
        in_specs=[
            pl.BlockSpec((None, tq, W), lambda b, h, i: (b, i, h)),
            pl.BlockSpec((None, T, W), lambda b, h, i: (b, 0, h)),
            pl.BlockSpec((None, T, W), lambda b, h, i: (b, 0, h)),
            pl.BlockSpec((None, tq, W), lambda b, h, i: (b, i, h)),
            pl.BlockSpec((tq, tq), lambda b, h, i: (0, 0)),
        ],
        out_specs=pl.BlockSpec((None, tq, W), lambda b, h, i: (b, i, h)),
        out_shape=jax.ShapeDtypeStruct((B, T, BR), BF16),
        scratch_shapes=[
            pltpu.VMEM((T, W), BF16),
        ],
        compiler_params=_params("arbitrary", "arbitrary", "arbitrary"),
        name="sb_prompt",
    )(q, k, v, gate, _tri(tq))


def _block_diag_q(q, dec_seq):
    DB = q.shape[0]
    t = q.reshape(DB, dec_seq, N_GROUPS, LANE).transpose(0, 2, 1, 3)
    eye = jnp.eye(N_GROUPS, dtype=q.dtype)
    bd = t[:, :, :, None, :] * eye[None, :, None, :, None]
    return bd.reshape(DB, N_GROUPS * dec_seq, N_GROUPS * LANE)


def _half_page_specs(layer, pages_of_step, page, lane_halves):
    def spec(s, half):
        def index(b, g, pt):
            p = pt[b, pages_of_step(g, s)]
            return (layer, p, 0, 0, half) if lane_halves else (layer, p, 0, half, 0)
        return pl.BlockSpec((None, None, page, SUBLANE, LANE), index)
    return [spec(s, half) for s in range(PAGES_PER_STEP) for half in range(2)]


def _regroup(dst_ref, slot, halves, column_of):
    page = halves[0].shape[0]
    for half, ref in enumerate(halves):
        rows = ref.reshape(page * SUBLANE, LANE)
        for r in range(SUBLANE):
            c = column_of(half, r)
            dst_ref[slot * page:(slot + 1) * page, c * LANE:(c + 1) * LANE] = (
                rows[pl.ds(r, page, stride=SUBLANE), :].astype(BF16))


def _new_page(new_ref, page):
    n = new_ref.shape[0]
    pad = jnp.zeros((page - n, new_ref.shape[1]), F32)
    return jnp.concatenate([new_ref[...], pad], axis=0).astype(BF16)


def _diff_sample_body(pt_ref, q_ref, *refs, lam_init, n_steps):
    P = PAGES_PER_STEP
    k_refs, v_refs = refs[:2 * P], refs[2 * P:4 * P]
    (knew_ref, vnew_ref, last_ref, new_ref, gate_ref, lq1, lk1, lq2, lk2, gsub_ref,
     o_ref, kp_ref, vp_ref, m_ref, l_ref, acc_ref, r_ref) = refs[4 * P:]
    g = pl.program_id(1)
    W = VDIM_A
    rows_h = acc_ref.shape[0] // N_HEADS_A
    dec_seq = rows_h // 2

    @pl.when((pl.program_id(0) == 0) & (g == 0))
    def _():
        kp_ref[1] = jnp.zeros(kp_ref.shape[1:], BF16)
        vp_ref[1] = jnp.zeros(vp_ref.shape[1:], BF16)

    @pl.when(g == 0)
    def _():
        m_ref[...] = jnp.full(m_ref.shape, NEG, F32)
        l_ref[...] = jnp.zeros(l_ref.shape, F32)
        acc_ref[...] = jnp.zeros(acc_ref.shape, F32)

    def regroup(half):
        for s in range(PAGES_PER_HALF):
            slot = half * PAGES_PER_HALF + s
            _regroup(kp_ref.at[half], s, k_refs[2 * slot:2 * slot + 2], lambda hf, r: hf * SUBLANE + r)
            _regroup(vp_ref.at[half], s, v_refs[2 * slot:2 * slot + 2], lambda hf, r: 2 * r + hf)

    def update(kpage, vpage, bias):
        s = _dot_nt(q_ref[...], kpage)
        if bias is not None:
            s = s + bias
        m_old = m_ref[...]
        m_new = jnp.maximum(m_old, jnp.max(s, axis=-1, keepdims=True))
        p = jnp.exp2(s - m_new)
        alpha = jnp.exp2(m_old - m_new)
        l_ref[...] = alpha * l_ref[...] + jnp.sum(p, axis=-1, keepdims=True)
        m_ref[...] = m_new
        pv = _dot(p.astype(BF16), vpage)
        for h in range(N_HEADS_A):
            r0 = h * rows_h
            acc_ref[r0:r0 + rows_h, :] = (alpha[r0:r0 + rows_h] * acc_ref[r0:r0 + rows_h, :]
                                          + pv[r0:r0 + rows_h, h * W:(h + 1) * W])

    update(kp_ref[1], vp_ref[1], jnp.where(g == 0, NEG, 0.0))
    regroup(0)
    regroup(1)
    update(kp_ref[0], vp_ref[0], None)

    @pl.when(g == n_steps - 1)
    def _():
        page = k_refs[0].shape[0]
        update(kp_ref[1], vp_ref[1], last_ref[...])
        update(_new_page(knew_ref, page), _new_page(vnew_ref, page), new_ref[...])
        r_ref[...] = acc_ref[...] / l_ref[...]
        lam = _lambda(lq1, lk1, lq2, lk2, lam_init)
        for h in range(N_HEADS_A):
            r0 = h * rows_h
            o = r_ref[r0:r0 + dec_seq, :] - lam * r_ref[r0 + dec_seq:r0 + rows_h, :]
            o_ref[:, h * W:(h + 1) * W] = _diff_epilogue(
                o, gate_ref[:, h * W:(h + 1) * W], gsub_ref[...], lam_init)


def _diff_sample(qbd, cache_k, cache_v, layer, page_table, knew, vnew, last, new, gate, lams, gsub,
                 *, lam_init):
    DB, rows, BR = qbd.shape
    dec_seq = knew.shape[1]
    page = cache_k.shape[2]
    n_pages = page_table.shape[1]
    P = PAGES_PER_STEP
    assert n_pages % P == 0
    n_steps = n_pages // P
    full = lambda shape: pl.BlockSpec(shape, lambda b, g, pt: (0,) * len(shape))
    per_seq = lambda r: pl.BlockSpec((None, r, BR), lambda b, g, pt: (b, 0, 0))
    forward = lambda g, s: g * P + s
    half_keys = PAGES_PER_HALF * page
    last = jnp.pad(last, ((0, 0), (half_keys - page, 0)))
    grid_spec = pltpu.PrefetchScalarGridSpec(
        num_scalar_prefetch=1,
        grid=(DB, n_steps),
        in_specs=[per_seq(rows)]
        + _half_page_specs(layer, forward, page, False) + _half_page_specs(layer, forward, page, True)
        + [per_seq(dec_seq), per_seq(dec_seq), full((rows, half_keys)), full((rows, page)), per_seq(dec_seq)]
        + [full((1, HEAD_DIM_A))] * 4 + [full((1, VDIM_A))],
        out_specs=per_seq(dec_seq),
        scratch_shapes=[
            pltpu.VMEM((2, half_keys, BR), BF16),
            pltpu.VMEM((2, half_keys, BR), BF16),
            pltpu.VMEM((rows, 1), F32),
            pltpu.VMEM((rows, 1), F32),
            pltpu.VMEM((rows, VDIM_A), F32),
            pltpu.VMEM((rows, VDIM_A), F32),
        ],
    )
    return pl.pallas_call(
        functools.partial(_diff_sample_body, lam_init=lam_init, n_steps=n_steps),
        grid_spec=grid_spec,
        out_shape=jax.ShapeDtypeStruct((DB, dec_seq, BR), F32),
        compiler_params=_params("arbitrary", "arbitrary"),
        name="diff_sample",
    )(page_table, qbd, *([cache_k] * (2 * P)), *([cache_v] * (2 * P)),
      knew, vnew, last, new, gate, *lams, gsub)


def _sb_sample_body(pt_ref, q_ref, *refs, n_steps):
    P = PAGES_PER_STEP
    k_refs, v_refs = refs[:2 * P], refs[2 * P:4 * P]
    (knew_ref, vnew_ref, mask_ref, gate_ref, tri_step_ref, tri_page_ref, o_ref,
     kp_ref, vp_ref, carry_ref, acc_ref) = refs[4 * P:]
    g = pl.program_id(1)
    W = HEAD_DIM_B
    dec_seq = o_ref.shape[0]
    n_pairs = N_HEADS_B // 2
    rows_p = 2 * dec_seq

    def update(kpage, vpage, mask, tri_ref, stale=None):
        z = _dot_nt(q_ref[...], kpage)
        sp = _softplus(z)
        if mask is not None:
            sp = sp * mask
        cum = _suffix_sum(sp, tri_ref[...])
        log_a = z - cum - carry_ref[...]
        if stale is not None:
            log_a = log_a + jnp.where(stale, NEG, 0.0)
        a = jnp.exp(log_a)
        if mask is not None:
            a = a * mask
        pv = _dot(a.astype(BF16), vpage)
        for hp in range(n_pairs):
            r0 = hp * rows_p
            acc_ref[r0:r0 + rows_p, :] += pv[r0:r0 + rows_p, hp * 2 * W:(hp + 1) * 2 * W]
        tot = cum[:, 0:1]
        if stale is not None:
            tot = tot * jnp.where(stale, 0.0, 1.0)
        carry_ref[...] += tot

    @pl.when((pl.program_id(0) == 0) & (g == 0))
    def _():
        kp_ref[1] = jnp.zeros(kp_ref.shape[1:], BF16)
        vp_ref[1] = jnp.zeros(vp_ref.shape[1:], BF16)

    @pl.when(g == 0)
    def _():
        carry_ref[...] = jnp.zeros(carry_ref.shape, F32)
        acc_ref[...] = jnp.zeros(acc_ref.shape, F32)
        page = k_refs[0].shape[0]
        update(_new_page(knew_ref, page), _new_page(vnew_ref, page), mask_ref[...], tri_page_ref)

    def regroup(half):
        for s in range(PAGES_PER_HALF):
            slot = half * PAGES_PER_HALF + s
            by_head = lambda hf, r: hf * SUBLANE + r
            _regroup(kp_ref.at[half], PAGES_PER_HALF - 1 - s, k_refs[2 * slot:2 * slot + 2], by_head)
            _regroup(vp_ref.at[half], PAGES_PER_HALF - 1 - s, v_refs[2 * slot:2 * slot + 2], by_head)

    update(kp_ref[1], vp_ref[1], None, tri_step_ref, stale=g == 0)
    regroup(0)
    regroup(1)
    update(kp_ref[0], vp_ref[0], None, tri_step_ref)

    @pl.when(g == n_steps - 1)
    def _():
        update(kp_ref[1], vp_ref[1], None, tri_step_ref)
        for hp in range(n_pairs):
            r0 = hp * rows_p
            for e in range(2):
                h = 2 * hp + e
                o = acc_ref[r0 + e * dec_seq:r0 + (e + 1) * dec_seq, e * W:(e + 1) * W]
                o_ref[:, h * W:(h + 1) * W] = o * _silu(gate_ref[:, h * W:(h + 1) * W].astype(F32))


def _sb_sample(qbd, cache_k, cache_v, layer, page_table, knew, vnew, gate):
    DB, rows, BR = qbd.shape
    dec_seq = knew.shape[1]
    page = cache_k.shape[2]
    n_pages = page_table.shape[1]
    P = PAGES_PER_STEP
    assert n_pages % P == 0
    n_steps = n_pages // P
    half_keys = PAGES_PER_HALF * page
    t_of_row = jnp.arange(rows, dtype=jnp.int32)[:, None] % dec_seq
    mask = (jnp.arange(page, dtype=jnp.int32)[None, :] < t_of_row).astype(F32)
    full = lambda shape: pl.BlockSpec(shape, lambda b, g, pt: (0,) * len(shape))
    per_seq = lambda r: pl.BlockSpec((None, r, BR), lambda b, g, pt: (b, 0, 0))
    backward = lambda g, s: n_pages - 1 - (g * P + s)
    grid_spec = pltpu.PrefetchScalarGridSpec(
        num_scalar_prefetch=1,
        grid=(DB, n_steps),
        in_specs=[per_seq(rows)]
        + _half_page_specs(layer, backward, page, False) + _half_page_specs(layer, backward, page, False)
        + [per_seq(dec_seq), per_seq(dec_seq), full((rows, page)), per_seq(dec_seq),
           full((half_keys, half_keys)), full((page, page))],
        out_specs=per_seq(dec_seq),
        scratch_shapes=[
            pltpu.VMEM((2, half_keys, BR), BF16),
            pltpu.VMEM((2, half_keys, BR), BF16),
            pltpu.VMEM((rows, 1), F32),
            pltpu.VMEM((rows, 2 * HEAD_DIM_B), F32),
        ],
    )
    return pl.pallas_call(
        functools.partial(_sb_sample_body, n_steps=n_steps),
        grid_spec=grid_spec,
        out_shape=jax.ShapeDtypeStruct((DB, dec_seq, BR), F32),
        compiler_params=_params("arbitrary", "arbitrary"),
        name="sb_sample",
    )(page_table, qbd, *([cache_k] * (2 * P)), *([cache_v] * (2 * P)), knew, vnew, mask, gate,
      _tri(half_keys), _tri(page))


def _trunk(x, caches, page_table, tiles, norm_pre, norm_post, w_in, w_out, lams, subln):
    B, T, D = x.shape
    depth = norm_pre.shape[0]
    near, last, new = tiles
    cache_k_diff, cache_v_diff, cache_k_sb, cache_v_sb = caches
    sample = page_table is not None
    xf = x.reshape(B * T, D)
    kd, vd, ks, vs = [], [], [], []
    for i in range(depth):
        j = i // 2
        diff = i % 2 == 0
        q_scale = HEAD_DIM_A ** -0.5 * LOG2E if diff else HEAD_DIM_B ** -0.5
        q, k, kb, v, gate = _inproj(xf, norm_pre[i], w_in[i], q_scale=q_scale,
                                    gate_dtype=F32 if sample else BF16)
        BR = q.shape[1]
        shp = lambda a: a.reshape(B, T, BR)
        k_attn = shp(kb).astype(F32) if sample else shp(kb)
        if diff:
            lam_init = _lam_init(i)
            lam_j = tuple(l[j].reshape(1, HEAD_DIM_A) for l in lams)
            gsub = subln[j].reshape(1, VDIM_A)
            if sample:
                o = _diff_sample(_block_diag_q(shp(q), T), cache_k_diff, cache_v_diff, j, page_table,
                                 k_attn, shp(v), last, new, shp(gate), lam_j, gsub, lam_init=lam_init)
            else:
                o = _diff_prompt(shp(q), k_attn, shp(v), shp(gate), near, lam_j, gsub,
                                 lam_init=lam_init)
            kd.append(k.reshape(B, T, N_HEADS_A, 2, HEAD_DIM_A))
            vd.append(v.reshape(B, T, N_HEADS_A, VDIM_A))
        else:
            if sample:
                o = _sb_sample(_block_diag_q(shp(q), T), cache_k_sb, cache_v_sb, j, page_table,
                               k_attn, shp(v), shp(gate))
            else:
                o = _sb_prompt(shp(q), k_attn, shp(v), shp(gate))
            ks.append(k.reshape(B, T, N_HEADS_B, HEAD_DIM_B))
            vs.append(v.reshape(B, T, N_HEADS_B, HEAD_DIM_B))
        xf = _outproj(o.reshape(B * T, BR), w_out[i], norm_post[i], xf)
    return xf.reshape(B, T, D), jnp.stack(kd), jnp.stack(vd), jnp.stack(ks), jnp.stack(vs)


def kernel(x_prompt, x_sample, cache_k_diff, cache_v_diff, cache_k_sb, cache_v_sb, page_table, rel_bias, norm_pre, norm_post, w_in_diff, w_out_diff, lambda_q1, lambda_k1, lambda_q2, lambda_k2, subln_diff, w_in_sb, w_out_sb):
    depth = norm_pre.shape[0]
    page = cache_k_diff.shape[2]
    dec_seq = x_sample.shape[1]
    w_in = [(w_in_diff if i % 2 == 0 else w_in_sb)[i // 2].astype(BF16) for i in range(depth)]
    w_out = [(w_out_diff if i % 2 == 0 else w_out_sb)[i // 2].astype(BF16) for i in range(depth)]
    kd_shape = cache_k_diff.shape
    caches = (cache_k_diff.reshape(kd_shape[:3] + (N_GROUPS, LANE)), cache_v_diff, cache_k_sb, cache_v_sb)
    near, last, new = _bias_tiles(rel_bias, tq=Q_TILE, page=page, dec_seq=dec_seq)
    rows = N_GROUPS * dec_seq
    tiles = (near, last.reshape(rows, page), new.reshape(rows, page))
    lams = (lambda_q1, lambda_k1, lambda_q2, lambda_k2)
    args = (tiles, norm_pre, norm_post, w_in, w_out, lams, subln_diff)
    y_p, kd_p, vd_p, ks_p, vs_p = _trunk(x_prompt, caches, None, *args)
    y_s, kd_s, vd_s, ks_s, vs_s = _trunk(x_sample, caches, page_table, *args)
    return (y_p, y_s, kd_p, vd_p, ks_p, vs_p, kd_s, vd_s, ks_s, vs_s)
```

```python
import functools
import math

import jax
import jax.numpy as jnp
from jax import lax
from jax.experimental import pallas as pl
from jax.experimental.pallas import tpu as pltpu

F32 = jnp.float32
BF16 = jnp.bfloat16

N_HEADS_A = 8
HEAD_DIM_A = 128
VDIM_A = 2 * HEAD_DIM_A
N_HEADS_B = 16
HEAD_DIM_B = 128
N_GROUPS = 16
N_BUCKETS = 32
MAX_EXACT = N_BUCKETS // 2
MAX_DISTANCE = 128
EPS = 1e-6
NEG = -1e30
LOG2E = math.log2(math.e)

LANE = 128
SUBLANE = 8
VMEM_LIMIT = 56 * 1024 * 1024

Q_TILE = 256
PAGES_PER_STEP = 8
PAGES_PER_HALF = PAGES_PER_STEP // 2
PROJ_ROWS = 512
PROJ_COLS = SUBLANE * LANE
OUT_ROWS = 512


def _params(*sem):
    return pltpu.CompilerParams(dimension_semantics=sem, vmem_limit_bytes=VMEM_LIMIT)


def _dot_nt(a, b):
    return lax.dot_general(a, b, (((1,), (1,)), ((), ())), preferred_element_type=F32)


def _dot(a, b):
    return jnp.dot(a, b, preferred_element_type=F32)


def _silu(g):
    return g * (1.0 / (1.0 + jnp.exp(-g)))


def _softplus(z):
    return jnp.maximum(z, 0.0) + jnp.log(1.0 + jnp.exp2(jnp.abs(z) * -LOG2E))


def _lam_init(layer):
    return 0.8 - 0.6 * math.exp(-0.3 * layer)


def _inproj_body(x_ref, g_ref, w_ref, q_ref, k_ref, kb_ref, v_ref, gate_ref, h_ref, *, n_sub, q_scale):
    j = pl.program_id(1)
    tm = x_ref.shape[0]

    @pl.when(j == 0)
    def _():
        x = x_ref[...]
        ms = jnp.mean(x * x, axis=-1, keepdims=True)
        h_ref[...] = (x * lax.rsqrt(ms + EPS) * g_ref[...]).astype(BF16)

    def project():
        return _dot(h_ref[...], w_ref[...])

    @pl.when(j < n_sub)
    def _():
        q_ref[...] = (project() * q_scale).astype(q_ref.dtype)

    @pl.when((j >= n_sub) & (j < 2 * n_sub))
    def _():
        k = project()
        kb_ref[...] = k.astype(BF16)
        rows = k_ref.reshape(tm * SUBLANE, LANE)
        for r in range(SUBLANE):
            rows[pl.ds(r, tm, stride=SUBLANE), :] = k[:, r * LANE:(r + 1) * LANE]

    @pl.when((j >= 2 * n_sub) & (j < 3 * n_sub))
    def _():
        v_ref[...] = project()

    @pl.when(j >= 3 * n_sub)
    def _():
        gate_ref[...] = project().astype(gate_ref.dtype)


def _inproj(x, g, w, *, q_scale, gate_dtype):
    R, D = x.shape
    BR = w.shape[1] // 4
    tm = min(PROJ_ROWS, R)
    tn = PROJ_COLS
    assert R % tm == 0 and BR % tn == 0 and tn == SUBLANE * LANE
    n_sub = BR // tn

    def out_map(group):
        return lambda i, j: (i, jnp.clip(j - group * n_sub, 0, n_sub - 1))

    def k_map(i, j):
        return (i, jnp.clip(j - n_sub, 0, n_sub - 1), 0)

    return pl.pallas_call(
        functools.partial(_inproj_body, n_sub=n_sub, q_scale=q_scale),
        grid=(R // tm, 4 * n_sub),
        in_specs=[
            pl.BlockSpec((tm, D), lambda i, j: (i, 0)),
            pl.BlockSpec((1, D), lambda i, j: (0, 0)),
            pl.BlockSpec((D, tn), lambda i, j: (0, j)),
        ],
        out_specs=[
            pl.BlockSpec((tm, tn), out_map(0)),
            pl.BlockSpec((tm, SUBLANE, LANE), k_map),
            pl.BlockSpec((tm, tn), out_map(1)),
            pl.BlockSpec((tm, tn), out_map(2)),
            pl.BlockSpec((tm, tn), out_map(3)),
        ],
        out_shape=[
            jax.ShapeDtypeStruct((R, BR), BF16),
            jax.ShapeDtypeStruct((R, BR // LANE, LANE), F32),
            jax.ShapeDtypeStruct((R, BR), BF16),
            jax.ShapeDtypeStruct((R, BR), F32),
            jax.ShapeDtypeStruct((R, BR), gate_dtype),
        ],
        scratch_shapes=[pltpu.VMEM((tm, D), BF16)],
        compiler_params=_params("arbitrary", "arbitrary"),
        name="inproj",
    )(x, g.reshape(1, D), w)


def _outproj_body(o_ref, w_ref, g_ref, x_ref, y_ref):
    o = _dot(o_ref[...].astype(BF16), w_ref[...])
    ms = jnp.mean(o * o, axis=-1, keepdims=True)
    y_ref[...] = x_ref[...] + o * lax.rsqrt(ms + EPS) * g_ref[...]


def _outproj(o, w, g, x):
    R, BR = o.shape
    D = w.shape[1]
    tm = min(OUT_ROWS, R)
    assert R % tm == 0
    return pl.pallas_call(
        _outproj_body,
        grid=(R // tm,),
        in_specs=[
            pl.BlockSpec((tm, BR), lambda i: (i, 0)),
            pl.BlockSpec((BR, D), lambda i: (0, 0)),
            pl.BlockSpec((1, D), lambda i: (0, 0)),
            pl.BlockSpec((tm, D), lambda i: (i, 0)),
        ],
        out_specs=pl.BlockSpec((tm, D), lambda i: (i, 0)),
        out_shape=jax.ShapeDtypeStruct((R, D), F32),
        compiler_params=_params("arbitrary"),
        name="outproj",
    )(o, w, g.reshape(1, D), x)


def _bias_body(rb_ref, near_ref, last_ref, new_ref, *, page, dec_seq):
    h = pl.program_id(0)

    def tile(shape, q_of_row, offset):
        row = lax.broadcasted_iota(jnp.int32, shape, 0)
        col = lax.broadcasted_iota(jnp.int32, shape, 1)
        d = q_of_row(row) + offset - col
        n = jnp.maximum(d, 0)
        nf = jnp.maximum(n, 1).astype(F32)
        large = MAX_EXACT + (jnp.log(nf / MAX_EXACT) / math.log(MAX_DISTANCE / MAX_EXACT)
                             * (N_BUCKETS - MAX_EXACT)).astype(jnp.int32)
        large = jnp.minimum(large, N_BUCKETS - 1)
        bucket = jnp.where(n < MAX_EXACT, n, large)
        out = jnp.zeros(shape, F32)
        for b in range(N_BUCKETS):
            out = jnp.where(bucket == b, rb_ref[h, b], out)
        out = (out - rb_ref[h, N_BUCKETS - 1]) * LOG2E
        return jnp.where(d >= 0, out, NEG)

    tq = near_ref.shape[0]
    near_ref[...] = tile((tq, 2 * tq), lambda r: r, tq)
    rows = last_ref.shape[0]
    last_ref[...] = tile((rows, page), lambda r: r % dec_seq, page)
    new_ref[...] = tile((rows, page), lambda r: r % dec_seq, 0)


def _bias_tiles(rel_bias, *, tq, page, dec_seq):
    H = rel_bias.shape[1]
    rows = 2 * dec_seq
    return pl.pallas_call(
        functools.partial(_bias_body, page=page, dec_seq=dec_seq),
        grid=(H,),
        in_specs=[pl.BlockSpec(memory_space=pltpu.SMEM)],
        out_specs=[
            pl.BlockSpec((None, tq, 2 * tq), lambda h: (h, 0, 0)),
            pl.BlockSpec((None, rows, page), lambda h: (h, 0, 0)),
            pl.BlockSpec((None, rows, page), lambda h: (h, 0, 0)),
        ],
        out_shape=[
            jax.ShapeDtypeStruct((H, tq, 2 * tq), F32),
            jax.ShapeDtypeStruct((H, rows, page), F32),
            jax.ShapeDtypeStruct((H, rows, page), F32),
        ],
        compiler_params=_params("arbitrary"),
        name="bias_tiles",
    )(rel_bias.T)


def _lambda(lq1, lk1, lq2, lk2, lam_init):
    s1 = jnp.sum(lq1[...] * lk1[...], axis=-1, keepdims=True)
    s2 = jnp.sum(lq2[...] * lk2[...], axis=-1, keepdims=True)
    return jnp.exp(s1) - jnp.exp(s2) + lam_init


def _diff_epilogue(o, gate, gsub, lam_init):
    ms = jnp.mean(o * o, axis=-1, keepdims=True)
    on = (o * lax.rsqrt(ms + EPS) * gsub) * (1.0 - lam_init)
    return on * _silu(gate.astype(F32))


def _per_query_tile(qi, n_tiles, fn):
    for j in range(n_tiles):
        pl.when(qi == j)(functools.partial(fn, j))


def _diff_prompt_body(q_ref, k_ref, v_ref, gate_ref, near_ref,
                      lq1, lk1, lq2, lk2, gsub_ref, o_ref, vb_ref, *, lam_init):
    qi = pl.program_id(2)
    tq = q_ref.shape[0]
    d = HEAD_DIM_A
    kb_ref = k_ref

    @pl.when(qi == 0)
    def _():
        vb_ref[...] = v_ref[...].astype(BF16)

    def softmax_map(j, hh, mp):
        nk = (j + 1) * tq
        nn = min(nk, 2 * tq)
        nf = nk - nn
        cols = slice(hh * VDIM_A + mp * d, hh * VDIM_A + (mp + 1) * d)
        vcols = slice(hh * VDIM_A, (hh + 1) * VDIM_A)
        qm = q_ref[:, cols]
        sn = _dot_nt(qm, kb_ref[nf:nk, cols]) + near_ref[hh, :, 2 * tq - nn:]
        m = jnp.max(sn, axis=-1, keepdims=True)
        if nf:
            sf = _dot_nt(qm, kb_ref[0:nf, cols])
            m = jnp.maximum(m, jnp.max(sf, axis=-1, keepdims=True))
        pn = jnp.exp2(sn - m)
        l = jnp.sum(pn, axis=-1, keepdims=True)
        o = _dot(pn.astype(BF16), vb_ref[nf:nk, vcols])
        if nf:
            pf = jnp.exp2(sf - m)
            l = l + jnp.sum(pf, axis=-1, keepdims=True)
            o = o + _dot(pf.astype(BF16), vb_ref[0:nf, vcols])
        return o / l

    def tile(j):
        lam = _lambda(lq1, lk1, lq2, lk2, lam_init)
        for hh in range(q_ref.shape[1] // VDIM_A):
            vcols = slice(hh * VDIM_A, (hh + 1) * VDIM_A)
            o = softmax_map(j, hh, 0) - lam * softmax_map(j, hh, 1)
            o_ref[:, vcols] = _diff_epilogue(
                o, gate_ref[:, vcols], gsub_ref[...], lam_init).astype(o_ref.dtype)

    _per_query_tile(qi, pl.num_programs(2), tile)


def _diff_prompt(q, k, v, gate, near, lams, gsub, *, lam_init):
    B, T, BR = q.shape
    tq = Q_TILE
    assert T % tq == 0
    PAIR = 4
    H, W = N_HEADS_A // PAIR, PAIR * VDIM_A
    vec = pl.BlockSpec((1, HEAD_DIM_A), lambda b, h, i: (0, 0))
    return pl.pallas_call(
        functools.partial(_diff_prompt_body, lam_init=lam_init),
        grid=(B, H, T // tq),
        in_specs=[
            pl.BlockSpec((None, tq, W), lambda b, h, i: (b, i, h)),
            pl.BlockSpec((None, T, W), lambda b, h, i: (b, 0, h)),
            pl.BlockSpec((None, T, W), lambda b, h, i: (b, 0, h)),
            pl.BlockSpec((None, tq, W), lambda b, h, i: (b, i, h)),
            pl.BlockSpec((PAIR, tq, 2 * tq), lambda b, h, i: (h, 0, 0)),
            vec, vec, vec, vec,
            pl.BlockSpec((1, VDIM_A), lambda b, h, i: (0, 0)),
        ],
        out_specs=pl.BlockSpec((None, tq, W), lambda b, h, i: (b, i, h)),
        out_shape=jax.ShapeDtypeStruct((B, T, BR), BF16),
        scratch_shapes=[
            pltpu.VMEM((T, W), BF16),
        ],
        compiler_params=_params("arbitrary", "arbitrary", "arbitrary"),
        name="diff_prompt",
    )(q, k, v, gate, near, *lams, gsub)


def _suffix_sum(sp, tri):
    return _dot(sp.astype(BF16), tri)


def _sb_prompt_body(q_ref, k_ref, v_ref, gate_ref, tri_ref, o_ref, vb_ref):
    qi = pl.program_id(2)
    tq = q_ref.shape[0]
    kb_ref = k_ref

    @pl.when(qi == 0)
    def _():
        vb_ref[...] = v_ref[...].astype(BF16)

    def head(j, cols):
        nk = (j + 1) * tq
        tri = tri_ref[...]
        row = lax.broadcasted_iota(jnp.int32, (tq, tq), 0)
        col = lax.broadcasted_iota(jnp.int32, (tq, tq), 1)
        mask = col < row
        z_all = _dot_nt(q_ref[:, cols], kb_ref[0:nk, cols])
        carry = None
        a_tiles = []
        for c in range(j, -1, -1):
            z = z_all[:, c * tq:(c + 1) * tq]
            sp = _softplus(z)
            if c == j:
                sp = jnp.where(mask, sp, 0.0)
            log_a = z - _suffix_sum(sp, tri)
            if carry is not None:
                log_a = log_a - carry
            a = jnp.exp(log_a)
            if c == j:
                a = jnp.where(mask, a, 0.0)
            a_tiles.append(a.astype(BF16))
            if c:
                tot = jnp.sum(sp, axis=-1, keepdims=True)
                carry = tot if carry is None else carry + tot
        acc = _dot(jnp.concatenate(a_tiles[::-1], axis=1), vb_ref[0:nk, cols])
        o_ref[:, cols] = (acc * _silu(gate_ref[:, cols].astype(F32))).astype(o_ref.dtype)

    def tile(j):
        for hh in range(q_ref.shape[1] // HEAD_DIM_B):
            head(j, slice(hh * HEAD_DIM_B, (hh + 1) * HEAD_DIM_B))

    _per_query_tile(qi, pl.num_programs(2), tile)


def _tri(n):
    return jnp.tril(jnp.ones((n, n), BF16))


def _sb_prompt(q, k, v, gate):
    B, T, BR = q.shape
    tq = Q_TILE
    assert T % tq == 0
    GROUP = 4
    H, W = N_HEADS_B // GROUP, GROUP * HEAD_DIM_B
    return pl.pallas_call(
        _sb_prompt_body,
        grid=(B, H, T // tq),
        in_specs=[
            pl.BlockSpec((None, tq, W), lambda b, h, i: (b, i, h)),
            pl.BlockSpec((None, T, W), lambda b, h, i: (b, 0, h)),
            pl.BlockSpec((None, T, W), lambda b, h, i: (b, 0, h)),
            pl.BlockSpec((None, tq, W), lambda b, h, i: (b, i, h)),
            pl.BlockSpec((tq, tq), lambda b, h, i: (0, 0)),
        ],
        out_specs=pl.BlockSpec((None, tq, W), lambda b, h, i: (b, i, h)),
        out_shape=jax.ShapeDtypeStruct((B, T, BR), BF16),
        scratch_shapes=[
            pltpu.VMEM((T, W), BF16),
        ],
        compiler_params=_params("arbitrary", "arbitrary", "arbitrary"),
        name="sb_prompt",
    )(q, k, v, gate, _tri(tq))


def _block_diag_q(q, dec_seq):
    DB = q.shape[0]
    t = q.reshape(DB, dec_seq, N_GROUPS, LANE).transpose(0, 2, 1, 3)
    eye = jnp.eye(N_GROUPS, dtype=q.dtype)
    bd = t[:, :, :, None, :] * eye[None, :, None, :, None]
    return bd.reshape(DB, N_GROUPS * dec_seq, N_GROUPS * LANE)


def _half_page_specs(layer, pages_of_step, page, lane_halves):
    def spec(s, half):
        def index(b, g, pt):
            p = pt[b, pages_of_step(g, s)]
            return (layer, p, 0, 0, half) if lane_halves else (layer, p, 0, half, 0)
        return pl.BlockSpec((None, None, page, SUBLANE, LANE), index)
    return [spec(s, half) for s in range(PAGES_PER_STEP) for half in range(2)]


def _regroup(dst_ref, slot, halves, column_of):
    page = halves[0].shape[0]
    for half, ref in enumerate(halves):
        rows = ref.reshape(page * SUBLANE, LANE)
        for r in range(SUBLANE):
            c = column_of(half, r)
            dst_ref[slot * page:(slot + 1) * page, c * LANE:(c + 1) * LANE] = (
                rows[pl.ds(r, page, stride=SUBLANE), :].astype(BF16))


def _new_page(new_ref, page):
    n = new_ref.shape[0]
    pad = jnp.zeros((page - n, new_ref.shape[1]), F32)
    return jnp.concatenate([new_ref[...], pad], axis=0).astype(BF16)


def _diff_sample_body(pt_ref, q_ref, *refs, lam_init, n_steps):
    P = PAGES_PER_STEP
    k_refs, v_refs = refs[:2 * P], refs[2 * P:4 * P]
    (knew_ref, vnew_ref, last_ref, new_ref, gate_ref, lq1, lk1, lq2, lk2, gsub_ref,
     o_ref, kp_ref, vp_ref, m_ref, l_ref, acc_ref, r_ref) = refs[4 * P:]
    g = pl.program_id(1)
    W = VDIM_A
    rows_h = acc_ref.shape[0] // N_HEADS_A
    dec_seq = rows_h // 2

    @pl.when((pl.program_id(0) == 0) & (g == 0))
    def _():
        kp_ref[1] = jnp.zeros(kp_ref.shape[1:], BF16)
        vp_ref[1] = jnp.zeros(vp_ref.shape[1:], BF16)

    @pl.when(g == 0)
    def _():
        m_ref[...] = jnp.full(m_ref.shape, NEG, F32)
        l_ref[...] = jnp.zeros(l_ref.shape, F32)
        acc_ref[...] = jnp.zeros(acc_ref.shape, F32)

    def regroup(half):
        for s in range(PAGES_PER_HALF):
            slot = half * PAGES_PER_HALF + s
            _regroup(kp_ref.at[half], s, k_refs[2 * slot:2 * slot + 2], lambda hf, r: hf * SUBLANE + r)
            _regroup(vp_ref.at[half], s, v_refs[2 * slot:2 * slot + 2], lambda hf, r: 2 * r + hf)

    def update(kpage, vpage, bias):
        s = _dot_nt(q_ref[...], kpage)
        if bias is not None:
            s = s + bias
        m_old = m_ref[...]
        m_new = jnp.maximum(m_old, jnp.max(s, axis=-1, keepdims=True))
        p = jnp.exp2(s - m_new)
        alpha = jnp.exp2(m_old - m_new)
        l_ref[...] = alpha * l_ref[...] + jnp.sum(p, axis=-1, keepdims=True)
        m_ref[...] = m_new
        pv = _dot(p.astype(BF16), vpage)
        for h in range(N_HEADS_A):
            r0 = h * rows_h
            acc_ref[r0:r0 + rows_h, :] = (alpha[r0:r0 + rows_h] * acc_ref[r0:r0 + rows_h, :]
                                          + pv[r0:r0 + rows_h, h * W:(h + 1) * W])

    update(kp_ref[1], vp_ref[1], jnp.where(g == 0, NEG, 0.0))
    regroup(0)
    regroup(1)
    update(kp_ref[0], vp_ref[0], None)

    @pl.when(g == n_steps - 1)
    def _():
        page = k_refs[0].shape[0]
        update(kp_ref[1], vp_ref[1], last_ref[...])
        update(_new_page(knew_ref, page), _new_page(vnew_ref, page), new_ref[...])
        r_ref[...] = acc_ref[...] / l_ref[...]
        lam = _lambda(lq1, lk1, lq2, lk2, lam_init)
        for h in range(N_HEADS_A):
            r0 = h * rows_h
            o = r_ref[r0:r0 + dec_seq, :] - lam * r_ref[r0 + dec_seq:r0 + rows_h, :]
            o_ref[:, h * W:(h + 1) * W] = _diff_epilogue(
                o, gate_ref[:, h * W:(h + 1) * W], gsub_ref[...], lam_init)


def _diff_sample(qbd, cache_k, cache_v, layer, page_table, knew, vnew, last, new, gate, lams, gsub,
                 *, lam_init):
    DB, rows, BR = qbd.shape
    dec_seq = knew.shape[1]
    page = cache_k.shape[2]
    n_pages = page_table.shape[1]
    P = PAGES_PER_STEP
    assert n_pages % P == 0
    n_steps = n_pages // P
    full = lambda shape: pl.BlockSpec(shape, lambda b, g, pt: (0,) * len(shape))
    per_seq = lambda r: pl.BlockSpec((None, r, BR), lambda b, g, pt: (b, 0, 0))
    forward = lambda g, s: g * P + s
    half_keys = PAGES_PER_HALF * page
    last = jnp.pad(last, ((0, 0), (half_keys - page, 0)))
    grid_spec = pltpu.PrefetchScalarGridSpec(
        num_scalar_prefetch=1,
        grid=(DB, n_steps),
        in_specs=[per_seq(rows)]
        + _half_page_specs(layer, forward, page, False) + _half_page_specs(layer, forward, page, True)
        + [per_seq(dec_seq), per_seq(dec_seq), full((rows, half_keys)), full((rows, page)), per_seq(dec_seq)]
        + [full((1, HEAD_DIM_A))] * 4 + [full((1, VDIM_A))],
        out_specs=per_seq(dec_seq),
        scratch_shapes=[
            pltpu.VMEM((2, half_keys, BR), BF16),
            pltpu.VMEM((2, half_keys, BR), BF16),
            pltpu.VMEM((rows, 1), F32),
            pltpu.VMEM((rows, 1), F32),
            pltpu.VMEM((rows, VDIM_A), F32),
            pltpu.VMEM((rows, VDIM_A), F32),
        ],
    )
    return pl.pallas_call(
        functools.partial(_diff_sample_body, lam_init=lam_init, n_steps=n_steps),
        grid_spec=grid_spec,
        out_shape=jax.ShapeDtypeStruct((DB, dec_seq, BR), F32),
        compiler_params=_params("arbitrary", "arbitrary"),
        name="diff_sample",
    )(page_table, qbd, *([cache_k] * (2 * P)), *([cache_v] * (2 * P)),
      knew, vnew, last, new, gate, *lams, gsub)


def _sb_sample_body(pt_ref, q_ref, *refs, n_steps):
    P = PAGES_PER_STEP
    k_refs, v_refs = refs[:2 * P], refs[2 * P:4 * P]
    (knew_ref, vnew_ref, mask_ref, gate_ref, tri_step_ref, tri_page_ref, o_ref,
     kp_ref, vp_ref, carry_ref, acc_ref) = refs[4 * P:]
    g = pl.program_id(1)
    W = HEAD_DIM_B
    dec_seq = o_ref.shape[0]
    n_pairs = N_HEADS_B // 2
    rows_p = 2 * dec_seq

    def update(kpage, vpage, mask, tri_ref, stale=None):
        z = _dot_nt(q_ref[...], kpage)
        sp = _softplus(z)
        if mask is not None:
            sp = sp * mask
        cum = _suffix_sum(sp, tri_ref[...])
        log_a = z - cum - carry_ref[...]
        if stale is not None:
            log_a = log_a + jnp.where(stale, NEG, 0.0)
        a = jnp.exp(log_a)
        if mask is not None:
            a = a * mask
        pv = _dot(a.astype(BF16), vpage)
        for hp in range(n_pairs):
            r0 = hp * rows_p
            acc_ref[r0:r0 + rows_p, :] += pv[r0:r0 + rows_p, hp * 2 * W:(hp + 1) * 2 * W]
        tot = cum[:, 0:1]
        if stale is not None:
            tot = tot * jnp.where(stale, 0.0, 1.0)
        carry_ref[...] += tot

    @pl.when((pl.program_id(0) == 0) & (g == 0))
    def _():
        kp_ref[1] = jnp.zeros(kp_ref.shape[1:], BF16)
        vp_ref[1] = jnp.zeros(vp_ref.shape[1:], BF16)

    @pl.when(g == 0)
    def _():
        carry_ref[...] = jnp.zeros(carry_ref.shape, F32)
        acc_ref[...] = jnp.zeros(acc_ref.shape, F32)
        page = k_refs[0].shape[0]
        update(_new_page(knew_ref, page), _new_page(vnew_ref, page), mask_ref[...], tri_page_ref)

    def regroup(half):
        for s in range(PAGES_PER_HALF):
            slot = half * PAGES_PER_HALF + s
            by_head = lambda hf, r: hf * SUBLANE + r
            _regroup(kp_ref.at[half], PAGES_PER_HALF - 1 - s, k_refs[2 * slot:2 * slot + 2], by_head)
            _regroup(vp_ref.at[half], PAGES_PER_HALF - 1 - s, v_refs[2 * slot:2 * slot + 2], by_head)

    update(kp_ref[1], vp_ref[1], None, tri_step_ref, stale=g == 0)
    regroup(0)
    regroup(1)
    update(kp_ref[0], vp_ref[0], None, tri_step_ref)

    @pl.when(g == n_steps - 1)
    def _():
        update(kp_ref[1], vp_ref[1], None, tri_step_ref)
        for hp in range(n_pairs):
            r0 = hp * rows_p
            for e in range(2):
                h = 2 * hp + e
                o = acc_ref[r0 + e * dec_seq:r0 + (e + 1) * dec_seq, e * W:(e + 1) * W]
                o_ref[:, h * W:(h + 1) * W] = o * _silu(gate_ref[:, h * W:(h + 1) * W].astype(F32))


def _sb_sample(qbd, cache_k, cache_v, layer, page_table, knew, vnew, gate):
    DB, rows, BR = qbd.shape
    dec_seq = knew.shape[1]
    page = cache_k.shape[2]
    n_pages = page_table.shape[1]
    P = PAGES_PER_STEP
    assert n_pages % P == 0
    n_steps = n_pages // P
    half_keys = PAGES_PER_HALF * page
    t_of_row = jnp.arange(rows, dtype=jnp.int32)[:, None] % dec_seq
    mask = (jnp.arange(page, dtype=jnp.int32)[None, :] < t_of_row).astype(F32)
    full = lambda shape: pl.BlockSpec(shape, lambda b, g, pt: (0,) * len(shape))
    per_seq = lambda r: pl.BlockSpec((None, r, BR), lambda b, g, pt: (b, 0, 0))
    backward = lambda g, s: n_pages - 1 - (g * P + s)
    grid_spec = pltpu.PrefetchScalarGridSpec(
        num_scalar_prefetch=1,
        grid=(DB, n_steps),
        in_specs=[per_seq(rows)]
        + _half_page_specs(layer, backward, page, False) + _half_page_specs(layer, backward, page, False)
        + [per_seq(dec_seq), per_seq(dec_seq), full((rows, page)), per_seq(dec_seq),
           full((half_keys, half_keys)), full((page, page))],
        out_specs=per_seq(dec_seq),
        scratch_shapes=[
            pltpu.VMEM((2, half_keys, BR), BF16),
            pltpu.VMEM((2, half_keys, BR), BF16),
            pltpu.VMEM((rows, 1), F32),
            pltpu.VMEM((rows, 2 * HEAD_DIM_B), F32),
        ],
    )
    return pl.pallas_call(
        functools.partial(_sb_sample_body, n_steps=n_steps),
        grid_spec=grid_spec,
        out_shape=jax.ShapeDtypeStruct((DB, dec_seq, BR), F32),
        compiler_params=_params("arbitrary", "arbitrary"),
        name="sb_sample",
    )(page_table, qbd, *([cache_k] * (2 * P)), *([cache_v] * (2 * P)), knew, vnew, mask, gate,
      _tri(half_keys), _tri(page))


def _trunk(x, caches, page_table, tiles, norm_pre, norm_post, w_in, w_out, lams, subln):
    B, T, D = x.shape
    depth = norm_pre.shape[0]
    near, last, new = tiles
    cache_k_diff, cache_v_diff, cache_k_sb, cache_v_sb = caches
    sample = page_table is not None
    xf = x.reshape(B * T, D)
    kd, vd, ks, vs = [], [], [], []
    for i in range(depth):
        j = i // 2
        diff = i % 2 == 0
        q_scale = HEAD_DIM_A ** -0.5 * LOG2E if diff else HEAD_DIM_B ** -0.5
        q, k, kb, v, gate = _inproj(xf, norm_pre[i], w_in[i], q_scale=q_scale,
                                    gate_dtype=F32 if sample else BF16)
        BR = q.shape[1]
        shp = lambda a: a.reshape(B, T, BR)
        k_attn = shp(kb).astype(F32) if sample else shp(kb)
        if diff:
            lam_init = _lam_init(i)
            lam_j = tuple(l[j].reshape(1, HEAD_DIM_A) for l in lams)
            gsub = subln[j].reshape(1, VDIM_A)
            if sample:
                o = _diff_sample(_block_diag_q(shp(q), T), cache_k_diff, cache_v_diff, j, page_table,
                                 k_attn, shp(v), last, new, shp(gate), lam_j, gsub, lam_init=lam_init)
            else:
                o = _diff_prompt(shp(q), k_attn, shp(v), shp(gate), near, lam_j, gsub,
                                 lam_init=lam_init)
            kd.append(k.reshape(B, T, N_HEADS_A, 2, HEAD_DIM_A))
            vd.append(v.reshape(B, T, N_HEADS_A, VDIM_A))
        else:
            if sample:
                o = _sb_sample(_block_diag_q(shp(q), T), cache_k_sb, cache_v_sb, j, page_table,
                               k_attn, shp(v), shp(gate))
            else:
                o = _sb_prompt(shp(q), k_attn, shp(v), shp(gate))
            ks.append(k.reshape(B, T, N_HEADS_B, HEAD_DIM_B))
            vs.append(v.reshape(B, T, N_HEADS_B, HEAD_DIM_B))
        xf = _outproj(o.reshape(B * T, BR), w_out[i], norm_post[i], xf)
    return xf.reshape(B, T, D), jnp.stack(kd), jnp.stack(vd), jnp.stack(ks), jnp.stack(vs)


def kernel(x_prompt, x_sample, cache_k_diff, cache_v_diff, cache_k_sb, cache_v_sb, page_table, rel_bias, norm_pre, norm_post, w_in_diff, w_out_diff, lambda_q1, lambda_k1, lambda_q2, lambda_k2, subln_diff, w_in_sb, w_out_sb):
    depth = norm_pre.shape[0]
    page = cache_k_diff.shape[2]
    dec_seq = x_sample.shape[1]
    w_in = [(w_in_diff if i % 2 == 0 else w_in_sb)[i // 2].astype(BF16) for i in range(depth)]
    w_out = [(w_out_diff if i % 2 == 0 else w_out_sb)[i // 2].astype(BF16) for i in range(depth)]
    kd_shape = cache_k_diff.shape
    caches = (cache_k_diff.reshape(kd_shape[:3] + (N_GROUPS, LANE)), cache_v_diff, cache_k_sb, cache_v_sb)
    near, last, new = _bias_tiles(rel_bias, tq=Q_TILE, page=page, dec_seq=dec_seq)
    rows = N_GROUPS * dec_seq
    tiles = (near, last.reshape(rows, page), new.reshape(rows, page))
    lams = (lambda_q1, lambda_k1, lambda_q2, lambda_k2)
    args = (tiles, norm_pre, norm_post, w_in, w_out, lams, subln_diff)
    y_p, kd_p, vd_p, ks_p, vs_p = _trunk(x_prompt, caches, None, *args)
    y_s, kd_s, vd_s, ks_s, vs_s = _trunk(x_sample, caches, page_table, *args)
    return (y_p, y_s, kd_p, vd_p, ks_p, vs_p, kd_s, vd_s, ks_s, vs_s)
```
